```python
import math
import jax, jax.numpy as jnp
from jax import lax
import numpy as np

D_MODEL = 2048
BATCH = 2
SEQ = 4096
DEPTH = 4
DEC_BATCH = 32
DEC_SEQ = 4
PAST_LEN = 16384
PAGE_SIZE = 128

N_EVEN = (DEPTH + 1) // 2
N_ODD = DEPTH // 2

RET_HEADS = 8
RET_DK = 128
RET_DV = 128
RET_CHUNK = 128
RET_WIDTH = RET_HEADS * RET_DV

SWA_Q_HEADS = 16
SWA_KV_HEADS = 2
SWA_GROUP = SWA_Q_HEADS // SWA_KV_HEADS
SWA_HD = 64
WINDOW = 128
SWA_BLOCK = 128
SWA_WIDTH = SWA_Q_HEADS * SWA_HD

EVEN_SPLITS = (RET_HEADS * RET_DK, RET_HEADS * RET_DK, RET_WIDTH, RET_WIDTH,
               SWA_WIDTH, SWA_KV_HEADS * SWA_HD, SWA_KV_HEADS * SWA_HD)
EVEN_IN = sum(EVEN_SPLITS)
MIX_WIDTH = RET_WIDTH + SWA_WIDTH

SSM_GROUP_CH = 16
SSM_GROUPS = D_MODEL // SSM_GROUP_CH
SSM_P = 64

D_FF = 5632
CONV_W = 3

NORM_EPS = 1e-6

kernel_name = "hybrid_retention_swa_s5_convffn_step"


def rmsnorm(x, g):
    xf = x.astype(jnp.float32)
    y = xf * lax.rsqrt(jnp.mean(xf * xf, axis=-1, keepdims=True) + NORM_EPS)
    return (y * g.astype(jnp.float32)).astype(x.dtype)


def alibi_slopes(n):
    return 2.0 ** (-8.0 * jnp.arange(1, n + 1, dtype=jnp.float32) / n)


def retention(q, k, v, s0):
    b, L = q.shape[0], q.shape[1]
    c = math.gcd(L, RET_CHUNK)
    nc = L // c
    lg = jnp.log(1.0 - 2.0 ** (-5.0 - jnp.arange(RET_HEADS, dtype=jnp.float32)))
    idx = jnp.arange(c, dtype=jnp.float32)
    diff = idx[:, None] - idx[None, :]
    intra = jnp.where(diff >= 0, jnp.exp(lg[:, None, None] * jnp.maximum(diff, 0.0)), 0.0)
    read = jnp.exp(lg[None, :] * (idx[:, None] + 1.0))
    write = jnp.exp(lg[None, :] * (c - 1.0 - idx[:, None]))
    chunk_decay = jnp.exp(lg * c)
    qf = q.astype(jnp.float32)
    kf = k.astype(jnp.float32) * (RET_DK ** -0.5)
    vf = v.astype(jnp.float32)

    def to_chunks(t):
        return t.reshape((b, nc, c) + t.shape[2:]).swapaxes(0, 1)

    def step(s, inp):
        qc, kc, vc = inp
        sc = jnp.einsum('bihd,bjhd->bhij', qc, kc) * intra[None]
        o = (jnp.einsum('bhij,bjhe->bihe', sc, vc)
             + jnp.einsum('bihd,bhde->bihe', qc, s) * read[None, :, :, None])
        s = (s * chunk_decay[None, :, None, None]
             + jnp.einsum('bjhd,bjhe->bhde', kc * write[None, :, :, None], vc))
        return s, o

    s_fin, o = lax.scan(step, s0.astype(jnp.float32), (to_chunks(qf), to_chunks(kf), to_chunks(vf)))
    o = o.swapaxes(0, 1).reshape(b, L, RET_HEADS, RET_DV)
    return o, s_fin


def swa_attention(q, k, v, q_pos, k_pos, sinks):
    dist = q_pos[:, :, None] - k_pos[:, None, :]
    valid = (dist >= 0) & (dist <= WINDOW) & (k_pos[:, None, :] >= 0)
    slopes = alibi_slopes(SWA_Q_HEADS).reshape(SWA_KV_HEADS, SWA_GROUP)
    s = jnp.einsum('bnqhgd,bnshd->bnhgqs', q.astype(jnp.float32), k.astype(jnp.float32)) * (SWA_HD ** -0.5)
    s = s - slopes[None, None, :, :, None, None] * dist[None, :, None, None].astype(jnp.float32)
    s = jnp.where(valid[None, :, None, None], s, -jnp.inf)
    sink = sinks.astype(jnp.float32).reshape(SWA_KV_HEADS, SWA_GROUP)[None, None, :, :, None, None]
    m = jnp.maximum(jnp.max(s, axis=-1, keepdims=True), sink)
    p = jnp.exp(s - m)
    denom = jnp.sum(p, axis=-1, keepdims=True) + jnp.exp(sink - m)
    return jnp.einsum('bnhgqs,bnshd->bnqhgd', p / denom, v.astype(jnp.float32))


def even_mixer(h, w_in, w_out, sinks, ret_state, win_k, win_v):
    b, L, _ = h.shape
    proj = h @ w_in
    split_at = np.cumsum(np.array(EVEN_SPLITS))[:-1].tolist()
    rq, rk, rv, rg, aq, ak, av = jnp.split(proj, split_at, axis=-1)

    if ret_state is None:
        ret_state = jnp.zeros((b, RET_HEADS, RET_DK, RET_DV), jnp.float32)
    o_ret, s_ret = retention(rq.reshape(b, L, RET_HEADS, RET_DK), rk.reshape(b, L, RET_HEADS, RET_DK),
                             rv.reshape(b, L, RET_HEADS, RET_DV), ret_state)
    o_ret = o_ret * lax.rsqrt(jnp.mean(o_ret * o_ret, axis=-1, keepdims=True) + NORM_EPS)
    o_ret = o_ret.reshape(b, L, RET_WIDTH) * jax.nn.silu(rg.astype(jnp.float32))

    q = aq.reshape(b, L, SWA_KV_HEADS, SWA_GROUP, SWA_HD)
    k = ak.reshape(b, L, SWA_KV_HEADS, SWA_HD)
    v = av.reshape(b, L, SWA_KV_HEADS, SWA_HD)
    if win_k is None:
        nb = L // SWA_BLOCK

        def blocks_with_prev(t):
            prev = jnp.concatenate([jnp.zeros_like(t[:, :SWA_BLOCK]), t[:, :L - SWA_BLOCK]], axis=1)
            shp = (b, nb, SWA_BLOCK) + t.shape[2:]
            return jnp.concatenate([prev.reshape(shp), t.reshape(shp)], axis=2)

        qb = q.reshape(b, nb, SWA_BLOCK, SWA_KV_HEADS, SWA_GROUP, SWA_HD)
        kb, vb = blocks_with_prev(k), blocks_with_prev(v)
        start = jnp.arange(nb, dtype=jnp.int32) * SWA_BLOCK
        q_pos = start[:, None] + jnp.arange(SWA_BLOCK, dtype=jnp.int32)[None]
        k_pos = start[:, None] - SWA_BLOCK + jnp.arange(2 * SWA_BLOCK, dtype=jnp.int32)[None]
        keep = min(WINDOW, L)
        new_k, new_v = k[:, L - keep:], v[:, L - keep:]
    else:
        win = win_k.shape[1]
        kc = jnp.concatenate([win_k.astype(k.dtype), k], axis=1)
        vc = jnp.concatenate([win_v.astype(v.dtype), v], axis=1)
        qb, kb, vb = q[:, None], kc[:, None], vc[:, None]
        q_pos = (win + jnp.arange(L, dtype=jnp.int32))[None]
        k_pos = jnp.arange(win + L, dtype=jnp.int32)[None]
        new_k, new_v = kc[:, L:], vc[:, L:]
    o_swa = swa_attention(qb, kb, vb, q_pos, k_pos, sinks).reshape(b, L, SWA_WIDTH)

    out = jnp.concatenate([o_ret, o_swa], axis=-1).astype(h.dtype) @ w_out
    return out, s_ret, new_k, new_v


def ssm_mixer(h, lam_re, lam_im, log_step, b_re, b_im, c_re, c_im, d_skip, w_glu, st_re, st_im):
    b, L, _ = h.shape
    u = h.astype(jnp.float32).reshape(b, L, SSM_GROUPS, SSM_GROUP_CH)
    lam = lax.complex(lam_re.astype(jnp.float32), lam_im.astype(jnp.float32))
    delta = jnp.exp(log_step.astype(jnp.float32))[:, None]
    lam_bar = jnp.exp(lam * delta)
    coef = (lam_bar - 1.0) / lam
    b_bar = coef[..., None] * lax.complex(b_re.astype(jnp.float32), b_im.astype(jnp.float32))
    bu = lax.complex(jnp.einsum('blgh,gph->blgp', u, jnp.real(b_bar)),
                     jnp.einsum('blgh,gph->blgp', u, jnp.imag(b_bar)))
    a = jnp.broadcast_to(lam_bar[None, None], (1, L, SSM_GROUPS, SSM_P))

    def combine(e1, e2):
        a1, b1 = e1
        a2, b2 = e2
        return a1 * a2, a2 * b1 + b2

    a_cum, hs = lax.associative_scan(combine, (a, bu), axis=1)
    if st_re is not None:
        h0 = lax.complex(st_re.astype(jnp.float32), st_im.astype(jnp.float32))
        hs = hs + a_cum * h0[:, None]
    y = (jnp.einsum('blgp,ghp->blgh', jnp.real(hs), c_re.astype(jnp.float32))
         - jnp.einsum('blgp,ghp->blgh', jnp.imag(hs), c_im.astype(jnp.float32)))
    y = y.reshape(b, L, D_MODEL) + d_skip.astype(jnp.float32) * h.astype(jnp.float32)
    z = jax.nn.gelu(y).astype(h.dtype)
    za, zb = jnp.split(z @ w_glu, 2, axis=-1)
    out = za * jax.nn.sigmoid(zb)
    h_last = hs[:, -1]
    return out, jnp.real(h_last), jnp.imag(h_last)


def conv_ffn(h, w_a, w_g, conv_w, conv_b, w_down, prev):
    b, L, _ = h.shape
    a = h @ w_a
    g = h @ w_g
    if prev is None:
        prev = jnp.zeros((b, CONV_W - 1, D_FF), a.dtype)
    padded = jnp.concatenate([prev.astype(a.dtype), a], axis=1)
    conv = conv_b.astype(a.dtype)
    for j in range(CONV_W):
        conv = conv + conv_w[j] * padded[:, j:j + L]
    out = (jax.nn.gelu(conv) * g) @ w_down
    return out, padded[:, L:]


def run_group(x, p, st):
    new_ret, new_wk, new_wv, new_sre, new_sim, new_conv = [], [], [], [], [], []
    for layer in range(DEPTH):
        i = layer // 2
        h = rmsnorm(x, p['norm_mix_pre'][layer])
        if layer % 2 == 0:
            mix, s_ret, wk, wv = even_mixer(
                h, p['w_in_even'][i], p['w_out_even'][i], p['swa_sinks'][i],
                None if st is None else st['ret'][i],
                None if st is None else st['win_k'][i],
                None if st is None else st['win_v'][i])
            new_ret.append(s_ret)
            new_wk.append(wk)
            new_wv.append(wv)
        else:
            mix, sre, sim = ssm_mixer(
                h, p['ssm_lam_re'][i], p['ssm_lam_im'][i], p['ssm_log_step'][i],
                p['ssm_b_re'][i], p['ssm_b_im'][i], p['ssm_c_re'][i], p['ssm_c_im'][i],
                p['ssm_d'][i], p['w_glu'][i],
                None if st is None else st['ssm_re'][i],
                None if st is None else st['ssm_im'][i])
            new_sre.append(sre)
            new_sim.append(sim)
        x = x + rmsnorm(mix.astype(x.dtype), p['norm_mix_post'][layer])
        h = rmsnorm(x, p['norm_ffn_pre'][layer])
        f, cs = conv_ffn(h, p['ffn_w_a'][layer], p['ffn_w_g'][layer], p['ffn_conv_w'][layer],
                         p['ffn_conv_b'][layer], p['ffn_w_down'][layer],
                         None if st is None else st['conv'][layer])
        new_conv.append(cs)
        x = x + rmsnorm(f.astype(x.dtype), p['norm_ffn_post'][layer])
    return (x, jnp.stack(new_ret), jnp.stack(new_wk), jnp.stack(new_wv),
            jnp.stack(new_sre), jnp.stack(new_sim), jnp.stack(new_conv))


def setup_inputs(seed: int = 0) -> dict:
    key = jax.random.key(seed)
    ks = iter(list(jax.random.split(key, 32)))

    def nrm(shape, scale):
        return jax.random.normal(next(ks), shape, jnp.float32) * scale

    win_buf = min(WINDOW, PAST_LEN)
    x_prompt = nrm((BATCH, SEQ, D_MODEL), 1.0)
    x_sample = nrm((DEC_BATCH, DEC_SEQ, D_MODEL), 1.0)
    state_ret = nrm((N_EVEN, DEC_BATCH, RET_HEADS, RET_DK, RET_DV), 1.0)
    cache_swa_k = nrm((N_EVEN, DEC_BATCH, win_buf, SWA_KV_HEADS, SWA_HD), 1.0)
    cache_swa_v = nrm((N_EVEN, DEC_BATCH, win_buf, SWA_KV_HEADS, SWA_HD), 1.0)
    state_ssm_re = nrm((N_ODD, DEC_BATCH, SSM_GROUPS, SSM_P), 0.1)
    state_ssm_im = nrm((N_ODD, DEC_BATCH, SSM_GROUPS, SSM_P), 0.1)
    state_ffn_conv = nrm((DEPTH, DEC_BATCH, CONV_W - 1, D_FF), 1.0)
    norm_mix_pre = 1.0 + nrm((DEPTH, D_MODEL), 0.02)
    norm_mix_post = 1.0 + nrm((DEPTH, D_MODEL), 0.02)
    norm_ffn_pre = 1.0 + nrm((DEPTH, D_MODEL), 0.02)
    norm_ffn_post = 1.0 + nrm((DEPTH, D_MODEL), 0.02)
    w_in_even = nrm((N_EVEN, D_MODEL, EVEN_IN), D_MODEL ** -0.5)
    w_out_even = nrm((N_EVEN, MIX_WIDTH, D_MODEL), MIX_WIDTH ** -0.5)
    swa_sinks = nrm((N_EVEN, SWA_Q_HEADS), 1.0)
    ssm_lam_re = -0.5 + nrm((N_ODD, SSM_GROUPS, SSM_P), 0.01)
    ssm_lam_im = (jnp.pi * jnp.arange(SSM_P, dtype=jnp.float32))[None, None] + nrm((N_ODD, SSM_GROUPS, SSM_P), 0.01)
    ssm_log_step = jax.random.uniform(next(ks), (N_ODD, SSM_GROUPS), jnp.float32,
                                      minval=math.log(1e-3), maxval=math.log(1e-1))
    ssm_b_re = nrm((N_ODD, SSM_GROUPS, SSM_P, SSM_GROUP_CH), (2.0 * SSM_GROUP_CH) ** -0.5)
    ssm_b_im = nrm((N_ODD, SSM_GROUPS, SSM_P, SSM_GROUP_CH), (2.0 * SSM_GROUP_CH) ** -0.5)
    ssm_c_re = nrm((N_ODD, SSM_GROUPS, SSM_GROUP_CH, SSM_P), SSM_P ** -0.5)
    ssm_c_im = nrm((N_ODD, SSM_GROUPS, SSM_GROUP_CH, SSM_P), SSM_P ** -0.5)
    ssm_d = nrm((N_ODD, D_MODEL), 1.0)
    w_glu = nrm((N_ODD, D_MODEL, 2 * D_MODEL), D_MODEL ** -0.5)
    ffn_w_a = nrm((DEPTH, D_MODEL, D_FF), D_MODEL ** -0.5)
    ffn_w_g = nrm((DEPTH, D_MODEL, D_FF), D_MODEL ** -0.5)
    ffn_conv_w = nrm((DEPTH, CONV_W, D_FF), CONV_W ** -0.5)
    ffn_conv_b = nrm((DEPTH, D_FF), 0.01)
    ffn_w_down = nrm((DEPTH, D_FF, D_MODEL), D_FF ** -0.5)
    return {
        'x_prompt': x_prompt, 'x_sample': x_sample,
        'state_ret': state_ret, 'cache_swa_k': cache_swa_k, 'cache_swa_v': cache_swa_v,
        'state_ssm_re': state_ssm_re, 'state_ssm_im': state_ssm_im, 'state_ffn_conv': state_ffn_conv,
        'norm_mix_pre': norm_mix_pre, 'norm_mix_post': norm_mix_post,
        'norm_ffn_pre': norm_ffn_pre, 'norm_ffn_post': norm_ffn_post,
        'w_in_even': w_in_even, 'w_out_even': w_out_even, 'swa_sinks': swa_sinks,
        'ssm_lam_re': ssm_lam_re, 'ssm_lam_im': ssm_lam_im, 'ssm_log_step': ssm_log_step,
        'ssm_b_re': ssm_b_re, 'ssm_b_im': ssm_b_im, 'ssm_c_re': ssm_c_re, 'ssm_c_im': ssm_c_im,
        'ssm_d': ssm_d, 'w_glu': w_glu,
        'ffn_w_a': ffn_w_a, 'ffn_w_g': ffn_w_g, 'ffn_conv_w': ffn_conv_w, 'ffn_conv_b': ffn_conv_b,
        'ffn_w_down': ffn_w_down,
    }


def reference(x_prompt, x_sample, state_ret, cache_swa_k, cache_swa_v, state_ssm_re, state_ssm_im,
              state_ffn_conv, norm_mix_pre, norm_mix_post, norm_ffn_pre, norm_ffn_post,
              w_in_even, w_out_even, swa_sinks, ssm_lam_re, ssm_lam_im, ssm_log_step,
              ssm_b_re, ssm_b_im, ssm_c_re, ssm_c_im, ssm_d, w_glu,
              ffn_w_a, ffn_w_g, ffn_conv_w, ffn_conv_b, ffn_w_down):
    params = dict(norm_mix_pre=norm_mix_pre, norm_mix_post=norm_mix_post,
                  norm_ffn_pre=norm_ffn_pre, norm_ffn_post=norm_ffn_post,
                  w_in_even=w_in_even, w_out_even=w_out_even, swa_sinks=swa_sinks,
                  ssm_lam_re=ssm_lam_re, ssm_lam_im=ssm_lam_im, ssm_log_step=ssm_log_step,
                  ssm_b_re=ssm_b_re, ssm_b_im=ssm_b_im, ssm_c_re=ssm_c_re, ssm_c_im=ssm_c_im,
                  ssm_d=ssm_d, w_glu=w_glu, ffn_w_a=ffn_w_a, ffn_w_g=ffn_w_g,
                  ffn_conv_w=ffn_conv_w, ffn_conv_b=ffn_conv_b, ffn_w_down=ffn_w_down)
    sample_state = dict(ret=state_ret, win_k=cache_swa_k, win_v=cache_swa_v,
                        ssm_re=state_ssm_re, ssm_im=state_ssm_im, conv=state_ffn_conv)
    y_prompt, ret_p, wk_p, wv_p, sre_p, sim_p, conv_p = run_group(x_prompt, params, None)
    y_sample, ret_s, wk_s, wv_s, sre_s, sim_s, conv_s = run_group(x_sample, params, sample_state)
    return (y_prompt, y_sample, ret_p, ret_s, wk_p, wk_s, wv_p, wv_s,
            sre_p, sre_s, sim_p, sim_s, conv_p, conv_s)
```

```python
import functools

import jax
import jax.numpy as jnp
from jax import lax
from jax.experimental import pallas as pl
from jax.experimental.pallas import tpu as pltpu

F32 = jnp.float32
BF16 = jnp.bfloat16

D_MODEL = 2048
DEPTH = 4
RET_HEADS = 8
RET_DK = 128
RET_DV = 128
RET_CHUNK = 128
RET_WIDTH = RET_HEADS * RET_DV
SWA_Q_HEADS = 16
SWA_KV_HEADS = 2
SWA_HD = 64
WINDOW = 128
SWA_BLOCK = 128
SWA_WIDTH = SWA_Q_HEADS * SWA_HD
EVEN_IN = 5376
SSM_GROUP_CH = 16
SSM_GROUPS = 128
SSM_P = 64
SSM_STATE = SSM_GROUPS * SSM_P
SSM_LANE_BLOCK = 1024
SSM_CH_BLOCK = 256
SSM_BLOCKS = SSM_STATE // SSM_LANE_BLOCK
D_FF = 5632
CONV_W = 3
NORM_EPS = 1e-6

LANES = 128
SUBLANES = 8
BF16_ROWS = 16
VMEM_LIMIT = 56 * 1024 * 1024


def _params(sem, vmem=None):
    return pltpu.CompilerParams(dimension_semantics=sem, vmem_limit_bytes=vmem)


def _rms(x, g):
    return x * lax.rsqrt(jnp.mean(x * x, axis=-1, keepdims=True) + NORM_EPS) * g


def _resident(shape, index_map):
    return pl.BlockSpec(shape, index_map, pipeline_mode=pl.Buffered(1))


def _norm_kernel(x_ref, g_ref, o_ref):
    o_ref[...] = _rms(x_ref[...], g_ref[...]).astype(o_ref.dtype)


def rmsnorm_bf16(x, g, tm):
    m, d = x.shape
    return pl.pallas_call(
        _norm_kernel,
        grid=(m // tm,),
        in_specs=[pl.BlockSpec((tm, d), lambda i: (i, 0)), pl.BlockSpec((1, d), lambda i: (0, 0))],
        out_specs=pl.BlockSpec((tm, d), lambda i: (i, 0)),
        out_shape=jax.ShapeDtypeStruct((m, d), BF16),
        compiler_params=_params(("parallel",)),
        name="rmsnorm",
    )(x, g.reshape(1, d))


def _matmul_kernel(x_ref, w_ref, o_ref):
    o_ref[...] = jnp.dot(x_ref[...], w_ref[...], preferred_element_type=F32)


def matmul_f32out(x, w, tm, tn):
    m, k = x.shape
    n = w.shape[1]
    return pl.pallas_call(
        _matmul_kernel,
        grid=(n // tn, m // tm),
        in_specs=[pl.BlockSpec((tm, k), lambda j, i: (i, 0)), pl.BlockSpec((k, tn), lambda j, i: (0, j))],
        out_specs=pl.BlockSpec((tm, tn), lambda j, i: (i, j)),
        out_shape=jax.ShapeDtypeStruct((m, n), F32),
        compiler_params=_params(("parallel", "parallel"), VMEM_LIMIT),
        name="in_proj",
    )(x, w)


def _retention_kernel(decay_ref, q_ref, k_ref, v_ref, g_ref, intra_ref, read_ref, write_ref, s0_ref,
                      o_ref, s_ref):
    ci = pl.program_id(1)

    @pl.when(ci == 0)
    def _():
        s_ref[...] = s0_ref[...]

    for h in range(RET_HEADS):
        sl = slice(h * RET_DK, (h + 1) * RET_DK)
        q = q_ref[0, :, sl].astype(BF16)
        k = k_ref[0, :, sl] * (RET_DK ** -0.5)
        v = v_ref[0, :, sl].astype(BF16)
        g = g_ref[0, :, sl]
        s = s_ref[0, h]
        sc = lax.dot_general(q, k.astype(BF16), (((1,), (1,)), ((), ())), preferred_element_type=F32) * intra_ref[h]
        o = (jnp.dot(sc.astype(BF16), v, preferred_element_type=F32)
             + jnp.dot(q, s.astype(BF16), preferred_element_type=F32) * read_ref[h])
        kw = (k * write_ref[h]).astype(BF16)
        s_ref[0, h] = s * decay_ref[h] + lax.dot_general(kw, v, (((0,), (0,)), ((), ())),
                                                         preferred_element_type=F32)
        o = o * lax.rsqrt(jnp.mean(o * o, axis=-1, keepdims=True) + NORM_EPS)
        o = o * (g * jax.nn.sigmoid(g))
        o_ref[0, :, sl] = o.astype(o_ref.dtype)


def _retention_tables(c_real, c_pad):
    lg = jnp.log(1.0 - 2.0 ** (-5.0 - jnp.arange(RET_HEADS, dtype=F32)))
    idx = jnp.arange(c_pad, dtype=F32)
    diff = idx[:, None] - idx[None, :]
    intra = jnp.where(diff >= 0, jnp.exp(lg[:, None, None] * jnp.maximum(diff, 0.0)), 0.0)
    read = jnp.exp(lg[:, None] * (idx[None, :] + 1.0))
    write = jnp.exp(lg[:, None] * (c_real - 1.0 - idx[None, :]))
    decay = jnp.exp(lg * c_real)
    bshape = (RET_HEADS, c_pad, RET_DV)
    return decay, intra, jnp.broadcast_to(read[:, :, None], bshape), jnp.broadcast_to(write[:, :, None], bshape)


def retention(proj, s0, c_real, c_pad):
    b, l, _ = proj.shape
    nc = l // c_pad
    decay, intra, read, write = _retention_tables(c_real, c_pad)

    def col(j):
        return pl.BlockSpec((1, c_pad, RET_WIDTH), lambda bi, ci: (bi, ci, j))

    state_spec = pl.BlockSpec((1, RET_HEADS, RET_DK, RET_DV), lambda bi, ci: (bi, 0, 0, 0))
    return pl.pallas_call(
        _retention_kernel,
        grid=(b, nc),
        in_specs=[
            pl.BlockSpec(memory_space=pltpu.SMEM),
            col(0), col(1), col(2), col(3),
            pl.BlockSpec((RET_HEADS, c_pad, c_pad), lambda bi, ci: (0, 0, 0)),
            pl.BlockSpec((RET_HEADS, c_pad, RET_DV), lambda bi, ci: (0, 0, 0)),
            pl.BlockSpec((RET_HEADS, c_pad, RET_DV), lambda bi, ci: (0, 0, 0)),
            state_spec,
        ],
        out_specs=[pl.BlockSpec((1, c_pad, RET_WIDTH), lambda bi, ci: (bi, ci, 0)), state_spec],
        out_shape=[jax.ShapeDtypeStruct((b, l, RET_WIDTH), BF16),
                   jax.ShapeDtypeStruct((b, RET_HEADS, RET_DK, RET_DV), F32)],
        compiler_params=_params(("parallel", "arbitrary"), VMEM_LIMIT),
        name="retention",
    )(decay, proj, proj, proj, proj, intra, read, write, s0)


def _swa_kernel(slope_ref, sink_ref, q_ref, kp_ref, vp_ref, kc_ref, vc_ref, o_ref, *, tq, q0_base, q0_step):
    q0 = q0_base + pl.program_id(1) * q0_step
    row_p = lax.broadcasted_iota(jnp.int32, (tq, WINDOW), 0)
    col_p = lax.broadcasted_iota(jnp.int32, (tq, WINDOW), 1)
    dist_p = row_p - col_p + WINDOW
    valid_p = (dist_p <= WINDOW) & (q0 - WINDOW + col_p >= 0)
    dist_pf = dist_p.astype(F32)
    row_c = lax.broadcasted_iota(jnp.int32, (tq, tq), 0)
    col_c = lax.broadcasted_iota(jnp.int32, (tq, tq), 1)
    dist_c = row_c - col_c
    valid_c = dist_c >= 0
    dist_cf = dist_c.astype(F32)
    lane_lo = lax.broadcasted_iota(jnp.int32, (tq, LANES), 1) < SWA_HD

    def both(ref):
        x = ref[0]
        return x.astype(BF16), pltpu.roll(x, SWA_HD, 1).astype(BF16)

    kp, kp_sw = both(kp_ref)
    vp, vp_sw = both(vp_ref)
    kc, kc_sw = both(kc_ref)
    vc, vc_sw = both(vc_ref)
    scale = SWA_HD ** -0.5
    nt = (((1,), (1,)), ((), ()))

    def attend(qh, kprev, kcur, vprev, vcur, h):
        slope = slope_ref[h]
        sink = sink_ref[h]
        sp = lax.dot_general(qh, kprev, nt, preferred_element_type=F32) * scale - slope * dist_pf
        sc = lax.dot_general(qh, kcur, nt, preferred_element_type=F32) * scale - slope * dist_cf
        sp = jnp.where(valid_p, sp, -jnp.inf)
        sc = jnp.where(valid_c, sc, -jnp.inf)
        m = jnp.maximum(jnp.maximum(jnp.max(sp, axis=-1, keepdims=True), jnp.max(sc, axis=-1, keepdims=True)), sink)
        pp = jnp.exp(sp - m)
        pc = jnp.exp(sc - m)
        denom = jnp.sum(pp, axis=-1, keepdims=True) + jnp.sum(pc, axis=-1, keepdims=True) + jnp.exp(sink - m)
        return (jnp.dot((pp / denom).astype(BF16), vprev, preferred_element_type=F32)
                + jnp.dot((pc / denom).astype(BF16), vcur, preferred_element_type=F32))

    pairs_per_kv = SWA_Q_HEADS // SWA_KV_HEADS // 2
    for pair in range(SWA_Q_HEADS // 2):
        first_kv = pair < pairs_per_kv
        k_lo = (kp, kc) if first_kv else (kp_sw, kc_sw)
        k_hi = (kp_sw, kc_sw) if first_kv else (kp, kc)
        v_lo = (vp, vc) if first_kv else (vp_sw, vc_sw)
        v_hi = (vp_sw, vc_sw) if first_kv else (vp, vc)
        qp = q_ref[0, :, pair * LANES:(pair + 1) * LANES]
        q_even = jnp.where(lane_lo, qp, 0.0).astype(BF16)
        q_odd = jnp.where(lane_lo, 0.0, qp).astype(BF16)
        o_even = attend(q_even, k_lo[0], k_lo[1], v_lo[0], v_lo[1], 2 * pair)
        o_odd = attend(q_odd, k_hi[0], k_hi[1], v_hi[0], v_hi[1], 2 * pair + 1)
        o_ref[0, :, pair * LANES:(pair + 1) * LANES] = jnp.where(lane_lo, o_even, o_odd).astype(o_ref.dtype)


def swa_attention(slopes, sinks, q_arr, q_spec, kv_arrs, kv_specs, b, nb, tq, q0_base, q0_step):
    return pl.pallas_call(
        functools.partial(_swa_kernel, tq=tq, q0_base=q0_base, q0_step=q0_step),
        grid=(b, nb),
        in_specs=[pl.BlockSpec(memory_space=pltpu.SMEM), pl.BlockSpec(memory_space=pltpu.SMEM), q_spec] + kv_specs,
        out_specs=pl.BlockSpec((1, tq, SWA_WIDTH), lambda bi, i: (bi, i, 0)),
        out_shape=jax.ShapeDtypeStruct((b, nb * tq, SWA_WIDTH), BF16),
        compiler_params=_params(("parallel", "parallel"), VMEM_LIMIT),
        name="swa",
    )(slopes, sinks, q_arr, *kv_arrs)


def _proj_res_kernel(*refs, n_pieces, glu, has_next):
    acts = refs[:n_pieces]
    pos = n_pieces
    w_main = refs[pos:pos + n_pieces]
    pos += n_pieces
    w_gate = ()
    if glu:
        w_gate = refs[pos:pos + n_pieces]
        pos += n_pieces
    x_ref, gpost_ref = refs[pos], refs[pos + 1]
    pos += 2
    gnext_ref = None
    if has_next:
        gnext_ref = refs[pos]
        pos += 1
    xo_ref = refs[pos]
    ho_ref = refs[pos + 1] if has_next else None

    def mm(ws):
        acc = jnp.dot(acts[0][...], ws[0][...], preferred_element_type=F32)
        for a, w in zip(acts[1:], ws[1:]):
            acc = acc + jnp.dot(a[...], w[...], preferred_element_type=F32)
        return acc

    y = mm(w_main)
    if glu:
        y = y * jax.nn.sigmoid(mm(w_gate))
    x_new = x_ref[...] + _rms(y, gpost_ref[...])
    xo_ref[...] = x_new
    if has_next:
        ho_ref[...] = _rms(x_new, gnext_ref[...]).astype(ho_ref.dtype)


def proj_residual(acts, w, w_row_blocks, glu, x, g_post, g_next, tm, name):
    m = x.shape[0]
    n_pieces = len(acts)
    has_next = g_next is not None
    in_specs = [pl.BlockSpec((tm, a.shape[1]), lambda i: (i, 0)) for a in acts]
    args = list(acts)
    for half in range(2 if glu else 1):
        for p in range(n_pieces):
            kp = acts[p].shape[1]
            in_specs.append(_resident((kp, D_MODEL), functools.partial(lambda i, rb, cb: (rb, cb), rb=w_row_blocks[p], cb=half)))
            args.append(w)
    in_specs += [pl.BlockSpec((tm, D_MODEL), lambda i: (i, 0)), pl.BlockSpec((1, D_MODEL), lambda i: (0, 0))]
    args += [x, g_post.reshape(1, D_MODEL)]
    out_specs = [pl.BlockSpec((tm, D_MODEL), lambda i: (i, 0))]
    out_shape = [jax.ShapeDtypeStruct((m, D_MODEL), F32)]
    if has_next:
        in_specs.append(pl.BlockSpec((1, D_MODEL), lambda i: (0, 0)))
        args.append(g_next.reshape(1, D_MODEL))
        out_specs.append(pl.BlockSpec((tm, D_MODEL), lambda i: (i, 0)))
        out_shape.append(jax.ShapeDtypeStruct((m, D_MODEL), BF16))
    outs = pl.pallas_call(
        functools.partial(_proj_res_kernel, n_pieces=n_pieces, glu=glu, has_next=has_next),
        grid=(m // tm,),
        in_specs=in_specs,
        out_specs=out_specs,
        out_shape=out_shape,
        compiler_params=_params(("parallel",), VMEM_LIMIT),
        name=name,
    )(*args)
    return (outs[0], outs[1]) if has_next else (outs[0], None)


def _ffn_up_kernel(x_ref, wa_ref, wg_ref, cw_ref, cb_ref, init_ref, act_ref, tail_ref, ext_ref, *,
                   tm, cr, rs, tiles_per_seq):
    t_in_seq = pl.program_id(1) % tiles_per_seq

    @pl.when(t_in_seq == 0)
    def _():
        ext_ref[0:cr, :] = init_ref[...]

    @pl.when(t_in_seq != 0)
    def _():
        ext_ref[0:cr, :] = ext_ref[tm:tm + cr, :]

    x = x_ref[...]
    a = jnp.dot(x, wa_ref[...], preferred_element_type=F32)
    g = jnp.dot(x, wg_ref[...], preferred_element_type=F32)
    ext_ref[cr:cr + tm, :] = a
    conv = cb_ref[...] + cw_ref[0:1, :] * ext_ref[cr - 2 * rs:cr - 2 * rs + tm, :]
    conv = conv + cw_ref[1:2, :] * ext_ref[cr - rs:cr - rs + tm, :]
    conv = conv + cw_ref[2:3, :] * a
    act_ref[...] = (jax.nn.gelu(conv) * g).astype(act_ref.dtype)

    @pl.when(t_in_seq == tiles_per_seq - 1)
    def _():
        tail_ref[0] = ext_ref[tm:tm + cr, :]


def ffn_up(h, wa, wg, conv_w, conv_b, init, tm, tf, rs, rows_per_seq):
    m = h.shape[0]
    cr = init.shape[0]
    tiles_per_seq = rows_per_seq // tm
    nseq = m // rows_per_seq
    return pl.pallas_call(
        functools.partial(_ffn_up_kernel, tm=tm, cr=cr, rs=rs, tiles_per_seq=tiles_per_seq),
        grid=(D_FF // tf, m // tm),
        in_specs=[
            pl.BlockSpec((tm, D_MODEL), lambda f, i: (i, 0)),
            pl.BlockSpec((D_MODEL, tf), lambda f, i: (0, f)),
            pl.BlockSpec((D_MODEL, tf), lambda f, i: (0, f)),
            pl.BlockSpec((CONV_W, tf), lambda f, i: (0, f)),
            pl.BlockSpec((1, tf), lambda f, i: (0, f)),
            pl.BlockSpec((cr, tf), lambda f, i: (0, f)),
        ],
        out_specs=[pl.BlockSpec((tm, tf), lambda f, i: (i, f)),
                   pl.BlockSpec((1, cr, tf), lambda f, i: (i // tiles_per_seq, 0, f))],
        out_shape=[jax.ShapeDtypeStruct((m, D_FF), BF16), jax.ShapeDtypeStruct((nseq, cr, D_FF), F32)],
        scratch_shapes=[pltpu.VMEM((cr + tm, tf), F32)],
        compiler_params=_params(("parallel", "arbitrary"), VMEM_LIMIT),
        name="ffn_up",
    )(h, wa, wg, conv_w, conv_b.reshape(1, D_FF), init)


def _s5_prep_kernel(lre_ref, lim_ref, step_ref, lre_x_ref, lim_x_ref, step_x_ref, bre_ref, bim_ref,
                    lbr_ref, lbi_ref, bbr_ref, bbi_ref):
    def lam_bar(lre, lim, log_step):
        delta = jnp.exp(log_step)
        mag = jnp.exp(lre * delta)
        return mag * jnp.cos(lim * delta), mag * jnp.sin(lim * delta)

    lbr, lbi = lam_bar(lre_ref[...], lim_ref[...], step_ref[...])
    lbr_ref[...] = lbr
    lbi_ref[...] = lbi
    lre, lim = lre_x_ref[...], lim_x_ref[...]
    xr, xi = lam_bar(lre, lim, step_x_ref[...])
    nr, ni = xr - 1.0, xi
    den = lre * lre + lim * lim
    cr = (nr * lre + ni * lim) / den
    ci = (ni * lre - nr * lim) / den
    br, bi = bre_ref[...], bim_ref[...]
    bbr_ref[...] = cr * br - ci * bi
    bbi_ref[...] = cr * bi + ci * br


def s5_prep(lam_re, lam_im, log_step, b_re, b_im):
    gp = (SSM_GROUPS, SSM_P)
    gph = (SSM_GROUPS, SSM_P * SSM_GROUP_CH)
    step = jnp.broadcast_to(log_step[:, None], gp)
    rep = lambda a: jnp.repeat(a, SSM_GROUP_CH, axis=1)
    full = lambda shp: pl.BlockSpec(shp, lambda: (0, 0))
    return pl.pallas_call(
        _s5_prep_kernel,
        in_specs=[full(gp)] * 3 + [full(gph)] * 5,
        out_specs=[full(gp), full(gp), full(gph), full(gph)],
        out_shape=[jax.ShapeDtypeStruct(gp, F32)] * 2 + [jax.ShapeDtypeStruct(gph, F32)] * 2,
        name="s5_prep",
    )(lam_re, lam_im, step, rep(lam_re), rep(lam_im), rep(step), b_re.reshape(gph), b_im.reshape(gph))


def _s5_kernel(x_ref, gpre_ref, d_ref, lbr_ref, lbi_ref, bdr_ref, bdi_ref, cdr_ref, cdi_ref, s0r_ref, s0i_ref,
               z_ref, sr_ref, si_ref, ur_ref, ui_ref, *, nb, tt):
    ti = pl.program_id(1)

    @pl.when(ti == 0)
    def _():
        sr_ref[...] = s0r_ref[...]
        si_ref[...] = s0i_ref[...]

    h = _rms(x_ref[...], gpre_ref[...])
    hb = h.astype(BF16)
    for kb in range(SSM_BLOCKS):
        ch = slice(kb * SSM_CH_BLOCK, (kb + 1) * SSM_CH_BLOCK)
        ln = slice(kb * SSM_LANE_BLOCK, (kb + 1) * SSM_LANE_BLOCK)
        ur_ref[...] = jnp.dot(hb[:, ch], bdr_ref[kb], preferred_element_type=F32)
        ui_ref[...] = jnp.dot(hb[:, ch], bdi_ref[kb], preferred_element_type=F32)
        a_re = jnp.broadcast_to(lbr_ref[:, ln], (nb, SSM_LANE_BLOCK))
        a_im = jnp.broadcast_to(lbi_ref[:, ln], (nb, SSM_LANE_BLOCK))

        def step(t, carry):
            s_re, s_im = carry
            rows = pl.ds(pl.multiple_of(t * nb, nb), nb)
            n_re = (a_re * s_re - a_im * s_im) + ur_ref[rows, :]
            n_im = (a_re * s_im + a_im * s_re) + ui_ref[rows, :]
            ur_ref[rows, :] = n_re
            ui_ref[rows, :] = n_im
            return n_re, n_im

        s_re, s_im = lax.fori_loop(0, tt, step, (sr_ref[0, :, ln], si_ref[0, :, ln]), unroll=min(tt, 8))
        sr_ref[0, :, ln] = s_re
        si_ref[0, :, ln] = s_im
        y = (jnp.dot(ur_ref[...].astype(BF16), cdr_ref[kb], preferred_element_type=F32)
             - jnp.dot(ui_ref[...].astype(BF16), cdi_ref[kb], preferred_element_type=F32))
        y = y + d_ref[:, ch] * h[:, ch]
        z_ref[:, ch] = jax.nn.gelu(y).astype(z_ref.dtype)


def s5_mixer(x, g_pre, d_skip, lbr, lbi, bdr, bdi, cdr, cdi, s0r, s0i, nb, tt):
    m = x.shape[0]
    ng = s0r.shape[0]
    rows = nb * tt
    ntt = m // (ng * rows)
    row = lambda shp: pl.BlockSpec(shp, lambda gi, ti: (0, 0))
    state = pl.BlockSpec((1, nb, SSM_STATE), lambda gi, ti: (gi, 0, 0))
    bd = _resident((SSM_BLOCKS, SSM_CH_BLOCK, SSM_LANE_BLOCK), lambda gi, ti: (0, 0, 0))
    cd = _resident((SSM_BLOCKS, SSM_LANE_BLOCK, SSM_CH_BLOCK), lambda gi, ti: (0, 0, 0))
    return pl.pallas_call(
        functools.partial(_s5_kernel, nb=nb, tt=tt),
        grid=(ng, ntt),
        in_specs=[pl.BlockSpec((rows, D_MODEL), lambda gi, ti: (gi * ntt + ti, 0)),
                  row((1, D_MODEL)), row((1, D_MODEL)), row((1, SSM_STATE)), row((1, SSM_STATE)),
                  bd, bd, cd, cd, state, state],
        out_specs=[pl.BlockSpec((rows, D_MODEL), lambda gi, ti: (gi * ntt + ti, 0)), state, state],
        out_shape=[jax.ShapeDtypeStruct((m, D_MODEL), BF16),
                   jax.ShapeDtypeStruct((ng, nb, SSM_STATE), F32), jax.ShapeDtypeStruct((ng, nb, SSM_STATE), F32)],
        scratch_shapes=[pltpu.VMEM((rows, SSM_LANE_BLOCK), F32), pltpu.VMEM((rows, SSM_LANE_BLOCK), F32)],
        compiler_params=_params(("parallel", "arbitrary"), VMEM_LIMIT),
        name="s5",
    )(x, g_pre.reshape(1, D_MODEL), d_skip.reshape(1, D_MODEL), lbr.reshape(1, SSM_STATE), lbi.reshape(1, SSM_STATE),
      bdr, bdi, cdr, cdi, s0r, s0i)


def _block_diag(w, rows_first):
    gpb = SSM_GROUPS // SSM_BLOCKS
    a, b = w.shape[1], w.shape[2]
    eye = jnp.eye(gpb, dtype=w.dtype)
    out = jnp.einsum('kgab,gf->kgafb', w.reshape(SSM_BLOCKS, gpb, a, b), eye)
    return out.reshape(SSM_BLOCKS, gpb * a, gpb * b)


def _slopes():
    return 2.0 ** (-8.0 * jnp.arange(1, SWA_Q_HEADS + 1, dtype=F32) / SWA_Q_HEADS)


def kernel(x_prompt, x_sample, state_ret, cache_swa_k, cache_swa_v, state_ssm_re, state_ssm_im, state_ffn_conv, norm_mix_pre, norm_mix_post, norm_ffn_pre, norm_ffn_post, w_in_even, w_out_even, swa_sinks, ssm_lam_re, ssm_lam_im, ssm_log_step, ssm_b_re, ssm_b_im, ssm_c_re, ssm_c_im, ssm_d, w_glu, ffn_w_a, ffn_w_g, ffn_conv_w, ffn_conv_b, ffn_w_down):
    pb, pl_len, _ = x_prompt.shape
    sb, sl_len, _ = x_sample.shape
    mp = pb * pl_len
    ms = sb * sl_len
    s_pad = BF16_ROWS
    carry_rows_p = SUBLANES
    carry_rows_s = (CONV_W - 1) * sb
    slopes = _slopes()

    xp = x_prompt.reshape(mp, D_MODEL)
    xs = x_sample.transpose(1, 0, 2).reshape(ms, D_MODEL)

    w_in = w_in_even.astype(BF16)
    w_out = w_out_even.astype(BF16)
    w_glu_b = w_glu.astype(BF16)
    w_a = ffn_w_a.astype(BF16)
    w_g = ffn_w_g.astype(BF16)
    w_down = ffn_w_down.astype(BF16)

    hp = rmsnorm_bf16(xp, norm_mix_pre[0], 512)
    hs = rmsnorm_bf16(xs, norm_mix_pre[0], ms)

    ret_p, ret_s, wk_p, wk_s, wv_p, wv_s = [], [], [], [], [], []
    sre_p, sre_s, sim_p, sim_s, conv_p, conv_s = [], [], [], [], [], []

    for layer in range(DEPTH):
        i = layer // 2
        if layer % 2 == 0:
            proj = matmul_f32out(hp, w_in[i], 1024, 768).reshape(pb, pl_len, EVEN_IN)
            o_ret, s_ret = retention(proj, jnp.zeros((pb, RET_HEADS, RET_DK, RET_DV), F32), RET_CHUNK, RET_CHUNK)
            kcol, vcol = (RET_WIDTH * 4 + SWA_WIDTH) // LANES, (RET_WIDTH * 4 + SWA_WIDTH) // LANES + 1
            blk = lambda c, prev: pl.BlockSpec(
                (1, SWA_BLOCK, LANES),
                (lambda bi, qi: (bi, jnp.maximum(qi - 1, 0), c)) if prev else (lambda bi, qi: (bi, qi, c)))
            o_swa = swa_attention(
                slopes, swa_sinks[i], proj,
                pl.BlockSpec((1, SWA_BLOCK, SWA_WIDTH), lambda bi, qi: (bi, qi, RET_WIDTH * 4 // SWA_WIDTH)),
                [proj, proj, proj, proj], [blk(kcol, True), blk(vcol, True), blk(kcol, False), blk(vcol, False)],
                pb, pl_len // SWA_BLOCK, SWA_BLOCK, 0, SWA_BLOCK)
            ret_p.append(s_ret)
            kv0 = RET_WIDTH * 4 + SWA_WIDTH
            kvw = SWA_KV_HEADS * SWA_HD
            wk_p.append(proj[:, pl_len - WINDOW:, kv0:kv0 + kvw].reshape(pb, WINDOW, SWA_KV_HEADS, SWA_HD))
            wv_p.append(proj[:, pl_len - WINDOW:, kv0 + kvw:kv0 + 2 * kvw].reshape(pb, WINDOW, SWA_KV_HEADS, SWA_HD))
            xp, hp = proj_residual([o_ret.reshape(mp, RET_WIDTH), o_swa.reshape(mp, SWA_WIDTH)], w_out[i], [0, 1],
                                   False, xp, norm_mix_post[layer], norm_ffn_pre[layer], 512, "out_proj")

            proj_s = matmul_f32out(hs, w_in[i], ms, 768)
            proj_bt = proj_s.reshape(sl_len, sb, EVEN_IN).transpose(1, 0, 2)
            proj_pad = jnp.pad(proj_bt, ((0, 0), (0, s_pad - sl_len), (0, 0)))
            o_ret_s, s_ret_s = retention(proj_pad, state_ret[i], sl_len, s_pad)
            win = cache_swa_k.shape[2]
            ck = cache_swa_k[i].reshape(sb, win, kvw)
            cv = cache_swa_v[i].reshape(sb, win, kvw)
            cache_spec = pl.BlockSpec((1, win, kvw), lambda bi, qi: (bi, 0, 0))
            cur = lambda c: pl.BlockSpec((1, s_pad, LANES), lambda bi, qi: (bi, 0, c))
            o_swa_s = swa_attention(
                slopes, swa_sinks[i], proj_pad,
                pl.BlockSpec((1, s_pad, SWA_WIDTH), lambda bi, qi: (bi, 0, RET_WIDTH * 4 // SWA_WIDTH)),
                [ck, cv, proj_pad, proj_pad], [cache_spec, cache_spec, cur(kcol), cur(vcol)],
                sb, 1, s_pad, win, 0)
            ret_s.append(s_ret_s)
            k_new = proj_bt[:, :, kv0:kv0 + kvw]
            v_new = proj_bt[:, :, kv0 + kvw:kv0 + 2 * kvw]
            wk_s.append(jnp.concatenate([ck, k_new], axis=1)[:, sl_len:].reshape(sb, win, SWA_KV_HEADS, SWA_HD))
            wv_s.append(jnp.concatenate([cv, v_new], axis=1)[:, sl_len:].reshape(sb, win, SWA_KV_HEADS, SWA_HD))
            tb = lambda o: o[:, :sl_len].transpose(1, 0, 2).reshape(ms, -1)
            xs, hs = proj_residual([tb(o_ret_s), tb(o_swa_s)], w_out[i], [0, 1], False, xs,
                                   norm_mix_post[layer], norm_ffn_pre[layer], ms, "out_proj_s")
        else:
            lbr, lbi, bbr, bbi = s5_prep(ssm_lam_re[i], ssm_lam_im[i], ssm_log_step[i], ssm_b_re[i], ssm_b_im[i])
            to_hp = lambda a: a.reshape(SSM_GROUPS, SSM_P, SSM_GROUP_CH).transpose(0, 2, 1)
            bdr = _block_diag(to_hp(bbr), True).astype(BF16)
            bdi = _block_diag(to_hp(bbi), True).astype(BF16)
            cdr = _block_diag(ssm_c_re[i].transpose(0, 2, 1), False).astype(BF16)
            cdi = _block_diag(ssm_c_im[i].transpose(0, 2, 1), False).astype(BF16)
            zeros_state = jnp.zeros((pb, 1, SSM_STATE), F32)
            zp, s_re, s_im = s5_mixer(xp, norm_mix_pre[layer], ssm_d[i], lbr, lbi, bdr, bdi, cdr, cdi,
                                      zeros_state, zeros_state, 1, 256)
            sre_p.append(s_re.reshape(pb, SSM_GROUPS, SSM_P))
            sim_p.append(s_im.reshape(pb, SSM_GROUPS, SSM_P))
            xp, hp = proj_residual([zp], w_glu_b[i], [0], True, xp, norm_mix_post[layer], norm_ffn_pre[layer],
                                   256, "glu_proj")
            zs, s_re_s, s_im_s = s5_mixer(xs, norm_mix_pre[layer], ssm_d[i], lbr, lbi, bdr, bdi, cdr, cdi,
                                          state_ssm_re[i].reshape(1, sb, SSM_STATE),
                                          state_ssm_im[i].reshape(1, sb, SSM_STATE), sb, sl_len)
            sre_s.append(s_re_s.reshape(sb, SSM_GROUPS, SSM_P))
            sim_s.append(s_im_s.reshape(sb, SSM_GROUPS, SSM_P))
            xs, hs = proj_residual([zs], w_glu_b[i], [0], True, xs, norm_mix_post[layer], norm_ffn_pre[layer],
                                   ms, "glu_proj_s")

        g_next = norm_mix_pre[layer + 1] if (layer + 1 < DEPTH and (layer + 1) % 2 == 0) else None
        act, tail = ffn_up(hp, w_a[layer], w_g[layer], ffn_conv_w[layer], ffn_conv_b[layer],
                           jnp.zeros((carry_rows_p, D_FF), F32), 512, 512, 1, pl_len)
        conv_p.append(tail[:, carry_rows_p - (CONV_W - 1):])
        xp, hp = proj_residual([act], w_down[layer], [0], False, xp, norm_ffn_post[layer], g_next, 256, "ffn_down")

        init_s = state_ffn_conv[layer].transpose(1, 0, 2).reshape(carry_rows_s, D_FF)
        act_s, tail_s = ffn_up(hs, w_a[layer], w_g[layer], ffn_conv_w[layer], ffn_conv_b[layer],
                               init_s, ms, 512, sb, ms)
        conv_s.append(tail_s.reshape(CONV_W - 1, sb, D_FF).transpose(1, 0, 2))
        xs, hs = proj_residual([act_s], w_down[layer], [0], False, xs, norm_ffn_post[layer], g_next, ms, "ffn_down_s")

    y_prompt = xp.reshape(pb, pl_len, D_MODEL)
    y_sample = xs.reshape(sl_len, sb, D_MODEL).transpose(1, 0, 2)
    return (y_prompt, y_sample, jnp.stack(ret_p), jnp.stack(ret_s), jnp.stack(wk_p), jnp.stack(wk_s),
            jnp.stack(wv_p), jnp.stack(wv_s), jnp.stack(sre_p), jnp.stack(sre_s), jnp.stack(sim_p),
            jnp.stack(sim_s), jnp.stack(conv_p), jnp.stack(conv_s))
```

```python
import functools

import jax
import jax.numpy as jnp
from jax import lax
from jax.experimental import pallas as pl
from jax.experimental.pallas import tpu as pltpu

F32 = jnp.float32
BF16 = jnp.bfloat16

D_MODEL = 2048
DEPTH = 4
RET_HEADS = 8
RET_DK = 128
RET_DV = 128
RET_CHUNK = 128
RET_WIDTH = RET_HEADS * RET_DV
SWA_Q_HEADS = 16
SWA_KV_HEADS = 2
SWA_HD = 64
WINDOW = 128
SWA_BLOCK = 128
SWA_WIDTH = SWA_Q_HEADS * SWA_HD
SWA_CHAINS = 4
EVEN_IN = 5376
Q_COL = RET_WIDTH * 4
KV_COL = Q_COL + SWA_WIDTH
KV_WIDTH = SWA_KV_HEADS * SWA_HD
SSM_GROUP_CH = 16
SSM_GROUPS = 128
SSM_P = 64
SSM_STATE = SSM_GROUPS * SSM_P
SSM_LANE_BLOCK = 1024
SSM_CH_BLOCK = 256
SSM_BLOCKS = SSM_STATE // SSM_LANE_BLOCK
SSM_HALF = SSM_LANE_BLOCK // 2
D_FF = 5632
CONV_W = 3
NORM_EPS = 1e-6

LANES = 128
SUBLANES = 8
BF16_ROWS = 16
VMEM_LIMIT = 56 * 1024 * 1024

SSM_SLABS = SSM_STATE // LANES
S5_TT = 256
S5_PITCH = S5_TT + 4
SLAB_LANE_TILE = (0, 4, 1, 5, 2, 6, 3, 7)
LANE_TILE_SLAB = (0, 2, 4, 6, 1, 3, 5, 7)


def _params(sem, vmem=VMEM_LIMIT):
    return pltpu.CompilerParams(dimension_semantics=sem, vmem_limit_bytes=vmem)


def _rms(x, g):
    return x * lax.rsqrt(jnp.mean(x * x, axis=-1, keepdims=True) + NORM_EPS) * g


def _resident(shape, index_map):
    return pl.BlockSpec(shape, index_map, pipeline_mode=pl.Buffered(1))


def _norm_kernel(x_ref, g_ref, o_ref):
    o_ref[...] = _rms(x_ref[...], g_ref[...]).astype(o_ref.dtype)


def rmsnorm_bf16(x, g, tm):
    m, d = x.shape
    return pl.pallas_call(
        _norm_kernel,
        grid=(m // tm,),
        in_specs=[pl.BlockSpec((tm, d), lambda i: (i, 0)), pl.BlockSpec((1, d), lambda i: (0, 0))],
        out_specs=pl.BlockSpec((tm, d), lambda i: (i, 0)),
        out_shape=jax.ShapeDtypeStruct((m, d), BF16),
        compiler_params=_params(("parallel",)),
        name="rmsnorm",
    )(x, g.reshape(1, d))


def _in_proj_kernel(x_ref, xs_ref, w_ref, wo_ref, o_ref, os_ref, wob_ref, wb_ref):
    @pl.when(pl.program_id(1) == 0)
    def _():
        wb_ref[...] = w_ref[...].astype(BF16)
        wob_ref[...] = wo_ref[...].astype(BF16)
        os_ref[...] = jnp.dot(xs_ref[...], wb_ref[...], preferred_element_type=F32)

    o_ref[...] = jnp.dot(x_ref[...], wb_ref[...], preferred_element_type=F32)


def in_proj(h, hs, w_in, w_out, li, tm, tn):
    m, ms = h.shape[0], hs.shape[0]
    n_tiles = EVEN_IN // tn
    slab = 512
    n_slabs = D_MODEL // slab
    assert n_slabs <= n_tiles
    slab_idx = lambda j, i: (li, jnp.minimum(j, n_slabs - 1), 0)
    return pl.pallas_call(
        _in_proj_kernel,
        grid=(n_tiles, m // tm),
        in_specs=[pl.BlockSpec((tm, D_MODEL), lambda j, i: (i, 0)),
                  pl.BlockSpec((ms, D_MODEL), lambda j, i: (0, 0)),
                  pl.BlockSpec((None, D_MODEL, tn), lambda j, i: (li, 0, j)),
                  pl.BlockSpec((None, slab, D_MODEL), slab_idx, pipeline_mode=pl.Buffered(1))],
        out_specs=[pl.BlockSpec((tm, tn), lambda j, i: (i, j)),
                   pl.BlockSpec((ms, tn), lambda j, i: (0, j)),
                   pl.BlockSpec((slab, D_MODEL), lambda j, i: (jnp.minimum(j, n_slabs - 1), 0))],
        out_shape=[jax.ShapeDtypeStruct((m, EVEN_IN), F32), jax.ShapeDtypeStruct((ms, EVEN_IN), F32),
                   jax.ShapeDtypeStruct((D_MODEL, D_MODEL), BF16)],
        scratch_shapes=[pltpu.VMEM((D_MODEL, tn), BF16)],
        compiler_params=_params(("arbitrary", "arbitrary")),
        name="in_proj",
    )(h, hs, w_in, w_out)


def _retention_kernel(decay_ref, q_ref, k_ref, v_ref, g_ref, intra_ref, read_ref, write_ref, s0_ref,
                      o_ref, s_ref):
    ci = pl.program_id(1)

    @pl.when(ci == 0)
    def _():
        s_ref[...] = s0_ref[...]

    for h in range(RET_HEADS):
        sl = slice(h * RET_DK, (h + 1) * RET_DK)
        q = q_ref[0, :, sl].astype(BF16)
        k = k_ref[0, :, sl] * (RET_DK ** -0.5)
        v = v_ref[0, :, sl].astype(BF16)
        g = g_ref[0, :, sl]
        s = s_ref[0, h]
        sc = lax.dot_general(q, k.astype(BF16), (((1,), (1,)), ((), ())), preferred_element_type=F32) * intra_ref[h]
        o = (jnp.dot(sc.astype(BF16), v, preferred_element_type=F32)
             + jnp.dot(q, s.astype(BF16), preferred_element_type=F32) * read_ref[h])
        kw = (k * write_ref[h]).astype(BF16)
        s_ref[0, h] = s * decay_ref[h] + lax.dot_general(kw, v, (((0,), (0,)), ((), ())),
                                                         preferred_element_type=F32)
        o = o * lax.rsqrt(jnp.mean(o * o, axis=-1, keepdims=True) + NORM_EPS)
        o = o * (g * jax.nn.sigmoid(g))
        o_ref[0, :, sl] = o.astype(o_ref.dtype)


def _retention_tables(c_real, c_pad):
    lg = jnp.log(1.0 - 2.0 ** (-5.0 - jnp.arange(RET_HEADS, dtype=F32)))
    idx = jnp.arange(c_pad, dtype=F32)
    diff = idx[:, None] - idx[None, :]
    intra = jnp.where(diff >= 0, jnp.exp(lg[:, None, None] * jnp.maximum(diff, 0.0)), 0.0)
    read = jnp.exp(lg[:, None] * (idx[None, :] + 1.0))
    write = jnp.exp(lg[:, None] * (c_real - 1.0 - idx[None, :]))
    decay = jnp.exp(lg * c_real)
    bshape = (RET_HEADS, c_pad, RET_DV)
    return decay, intra, jnp.broadcast_to(read[:, :, None], bshape), jnp.broadcast_to(write[:, :, None], bshape)


def retention(proj, s0, c_real, c_pad):
    b, l, _ = proj.shape
    nc = l // c_pad
    decay, intra, read, write = _retention_tables(c_real, c_pad)

    def col(j):
        return pl.BlockSpec((1, c_pad, RET_WIDTH), lambda bi, ci: (bi, ci, j))

    state_spec = pl.BlockSpec((1, RET_HEADS, RET_DK, RET_DV), lambda bi, ci: (bi, 0, 0, 0))
    return pl.pallas_call(
        _retention_kernel,
        grid=(b, nc),
        in_specs=[
            pl.BlockSpec(memory_space=pltpu.SMEM),
            col(0), col(1), col(2), col(3),
            pl.BlockSpec((RET_HEADS, c_pad, c_pad), lambda bi, ci: (0, 0, 0)),
            pl.BlockSpec((RET_HEADS, c_pad, RET_DV), lambda bi, ci: (0, 0, 0)),
            pl.BlockSpec((RET_HEADS, c_pad, RET_DV), lambda bi, ci: (0, 0, 0)),
            state_spec,
        ],
        out_specs=[pl.BlockSpec((1, c_pad, RET_WIDTH), lambda bi, ci: (bi, ci, 0)), state_spec],
        out_shape=[jax.ShapeDtypeStruct((b, l, RET_WIDTH), BF16),
                   jax.ShapeDtypeStruct((b, RET_HEADS, RET_DK, RET_DV), F32)],
        compiler_params=_params(("parallel", "arbitrary")),
        name="retention",
    )(decay, proj, proj, proj, proj, intra, read, write, s0)


def _swa_kernel(slope_ref, sink_ref, q_ref, kp_ref, vp_ref, kc_ref, vc_ref, o_ref, *, tq, q0_base, q0_step):
    rows = (SWA_Q_HEADS // SWA_CHAINS) * tq
    q0 = q0_base + pl.program_id(1) * q0_step
    row_p = lax.broadcasted_iota(jnp.int32, (rows, WINDOW), 0) & (tq - 1)
    col_p = lax.broadcasted_iota(jnp.int32, (rows, WINDOW), 1)
    dist_p = row_p - col_p + WINDOW
    valid_p = (dist_p <= WINDOW) & (q0 - WINDOW + col_p >= 0)
    dist_pf = dist_p.astype(F32)
    row_c = lax.broadcasted_iota(jnp.int32, (rows, tq), 0) & (tq - 1)
    col_c = lax.broadcasted_iota(jnp.int32, (rows, tq), 1)
    dist_c = row_c - col_c
    valid_c = dist_c >= 0
    dist_cf = dist_c.astype(F32)
    lane_lo = lax.broadcasted_iota(jnp.int32, (tq, LANES), 1) < SWA_HD

    def both(ref):
        x = ref[0]
        return x.astype(BF16), pltpu.roll(x, SWA_HD, 1).astype(BF16)

    kp, kp_sw = both(kp_ref)
    vp, vp_sw = both(vp_ref)
    kc, kc_sw = both(kc_ref)
    vc, vc_sw = both(vc_ref)
    scale = SWA_HD ** -0.5
    nt = (((1,), (1,)), ((), ()))
    pairs_per_kv = SWA_Q_HEADS // SWA_KV_HEADS // 2

    for kv in range(SWA_KV_HEADS):
        outs = []
        for parity in range(2):
            chain = kv * 2 + parity
            swapped = (kv == 0) != (parity == 0)
            kprev, kcur = (kp_sw, kc_sw) if swapped else (kp, kc)
            vprev, vcur = (vp_sw, vc_sw) if swapped else (vp, vc)
            qs = []
            for p in range(pairs_per_kv):
                pair = kv * pairs_per_kv + p
                qp = q_ref[0, :, pair * LANES:(pair + 1) * LANES]
                qs.append(jnp.where(lane_lo, qp, 0.0) if parity == 0 else jnp.where(lane_lo, 0.0, qp))
            qh = jnp.concatenate(qs, axis=0).astype(BF16)
            slope = slope_ref[chain]
            sink = sink_ref[chain][:, :1]
            sp = lax.dot_general(qh, kprev, nt, preferred_element_type=F32) * scale - slope * dist_pf
            sc = lax.dot_general(qh, kcur, nt, preferred_element_type=F32) * scale - slope[:, :tq] * dist_cf
            sp = jnp.where(valid_p, sp, -jnp.inf)
            sc = jnp.where(valid_c, sc, -jnp.inf)
            m = jnp.maximum(jnp.maximum(jnp.max(sp, axis=-1, keepdims=True), jnp.max(sc, axis=-1, keepdims=True)),
                            sink)
            pp = jnp.exp(sp - m)
            pc = jnp.exp(sc - m)
            denom = jnp.sum(pp, axis=-1, keepdims=True) + jnp.sum(pc, axis=-1, keepdims=True) + jnp.exp(sink - m)
            outs.append(jnp.dot((pp / denom).astype(BF16), vprev, preferred_element_type=F32)
                        + jnp.dot((pc / denom).astype(BF16), vcur, preferred_element_type=F32))
        for p in range(pairs_per_kv):
            pair = kv * pairs_per_kv + p
            rs = slice(p * tq, (p + 1) * tq)
            o_ref[0, :, pair * LANES:(pair + 1) * LANES] = jnp.where(lane_lo, outs[0][rs], outs[1][rs]).astype(o_ref.dtype)


def _swa_table(vals, tq):
    per = SWA_Q_HEADS // SWA_CHAINS
    v = vals.astype(F32).reshape(SWA_KV_HEADS, per, 2).transpose(0, 2, 1).reshape(SWA_CHAINS, per)
    return jnp.broadcast_to(v[:, :, None, None], (SWA_CHAINS, per, tq, LANES)).reshape(SWA_CHAINS, per * tq, LANES)


def swa_attention(slopes, sinks, q_arr, q_spec, kv_arrs, kv_specs, b, nb, tq, q0_base, q0_step):
    table = pl.BlockSpec((SWA_CHAINS, (SWA_Q_HEADS // SWA_CHAINS) * tq, LANES), lambda bi, i: (0, 0, 0))
    return pl.pallas_call(
        functools.partial(_swa_kernel, tq=tq, q0_base=q0_base, q0_step=q0_step),
        grid=(b, nb),
        in_specs=[table, table, q_spec] + kv_specs,
        out_specs=pl.BlockSpec((1, tq, SWA_WIDTH), lambda bi, i: (bi, i, 0)),
        out_shape=jax.ShapeDtypeStruct((b, nb * tq, SWA_WIDTH), BF16),
        compiler_params=_params(("parallel", "parallel")),
        name="swa",
    )(_swa_table(slopes, tq), _swa_table(sinks, tq), q_arr, *kv_arrs)


def _res_tile(acts, w_main, w_gate, x_ref, gpost_ref, gnext_ref, xo_ref, ho_ref):
    def mm(ws):
        acc = jnp.dot(acts[0][...], ws[0][...], preferred_element_type=F32)
        for a, w in zip(acts[1:], ws[1:]):
            acc = acc + jnp.dot(a[...], w[...], preferred_element_type=F32)
        return acc

    y = mm(w_main)
    if w_gate:
        y = y * jax.nn.sigmoid(mm(w_gate))
    x_new = x_ref[...] + _rms(y, gpost_ref[...])
    xo_ref[...] = x_new
    if ho_ref is not None:
        ho_ref[...] = _rms(x_new, gnext_ref[...]).astype(ho_ref.dtype)


def _proj_res_kernel(*refs, n_pieces, glu, has_next, has_cast):
    it = iter(refs)
    take = lambda k: [next(it) for _ in range(k)]
    acts, acts_s, w_main = take(n_pieces), take(n_pieces), take(n_pieces)
    w_gate = take(n_pieces) if glu else []
    x_ref, xs_ref, gpost_ref = take(3)
    gnext_ref = next(it) if has_next else None
    src_ref = next(it) if has_cast else None
    xo_ref, xos_ref = take(2)
    ho_ref, hos_ref = take(2) if has_next else (None, None)
    dst_ref = next(it) if has_cast else None

    _res_tile(acts, w_main, w_gate, x_ref, gpost_ref, gnext_ref, xo_ref, ho_ref)

    @pl.when(pl.program_id(0) == 0)
    def _():
        _res_tile(acts_s, w_main, w_gate, xs_ref, gpost_ref, gnext_ref, xos_ref, hos_ref)

    if has_cast:
        dst_ref[...] = src_ref[...].astype(dst_ref.dtype)


def proj_residual(acts, acts_s, w, w_row_blocks, glu, x, xs, g_post, g_next, tm, name, cast=None):
    m, ms = x.shape[0], xs.shape[0]
    steps = m // tm
    n_pieces = len(acts)
    has_next = g_next is not None
    row = lambda i: (i, 0)
    fixed = lambda i: (0, 0)
    in_specs = [pl.BlockSpec((tm, a.shape[1]), row) for a in acts]
    in_specs += [pl.BlockSpec((ms, a.shape[1]), fixed) for a in acts_s]
    args = list(acts) + list(acts_s)
    for half in range(2 if glu else 1):
        for p in range(n_pieces):
            in_specs.append(_resident((acts[p].shape[1], D_MODEL),
                                      functools.partial(lambda i, rb, cb: (rb, cb), rb=w_row_blocks[p], cb=half)))
            args.append(w)
    in_specs += [pl.BlockSpec((tm, D_MODEL), row), pl.BlockSpec((ms, D_MODEL), fixed), pl.BlockSpec((1, D_MODEL), fixed)]
    args += [x, xs, g_post.reshape(1, D_MODEL)]
    if has_next:
        in_specs.append(pl.BlockSpec((1, D_MODEL), fixed))
        args.append(g_next.reshape(1, D_MODEL))
    if cast is not None:
        src, li = cast
        slab = src.shape[1] // steps
        in_specs.append(pl.BlockSpec((None, slab, src.shape[2]), lambda i: (li, i, 0)))
        args.append(src)
    out_specs = [pl.BlockSpec((tm, D_MODEL), row), pl.BlockSpec((ms, D_MODEL), fixed)]
    out_shape = [jax.ShapeDtypeStruct((m, D_MODEL), F32), jax.ShapeDtypeStruct((ms, D_MODEL), F32)]
    if has_next:
        out_specs += [pl.BlockSpec((tm, D_MODEL), row), pl.BlockSpec((ms, D_MODEL), fixed)]
        out_shape += [jax.ShapeDtypeStruct((m, D_MODEL), BF16), jax.ShapeDtypeStruct((ms, D_MODEL), BF16)]
    if cast is not None:
        out_specs.append(pl.BlockSpec((slab, src.shape[2]), row))
        out_shape.append(jax.ShapeDtypeStruct(src.shape[1:], BF16))
    outs = pl.pallas_call(
        functools.partial(_proj_res_kernel, n_pieces=n_pieces, glu=glu, has_next=has_next, has_cast=cast is not None),
        grid=(steps,),
        in_specs=in_specs,
        out_specs=out_specs,
        out_shape=out_shape,
        compiler_params=_params(("arbitrary",)),
        name=name,
    )(*args)
    outs = list(outs)
    x_new, xs_new = outs[0], outs[1]
    h_new, hs_new = (outs[2], outs[3]) if has_next else (None, None)
    w_cast = outs[-1] if cast is not None else None
    return x_new, xs_new, h_new, hs_new, w_cast


def _ffn_rows(x_ref, wa_ref, wg_ref, cw_ref, cb_ref, ext_ref, act_ref, *, rows, sub, cr, rs):
    for s in range(rows // sub):
        r0 = s * sub
        x = x_ref[r0:r0 + sub, :]
        a = jnp.dot(x, wa_ref[...], preferred_element_type=F32)
        g = jnp.dot(x, wg_ref[...], preferred_element_type=F32)
        ext_ref[cr + r0:cr + r0 + sub, :] = a
        conv = cb_ref[...] + cw_ref[0:1, :] * ext_ref[cr - 2 * rs + r0:cr - 2 * rs + r0 + sub, :]
        conv = conv + cw_ref[1:2, :] * ext_ref[cr - rs + r0:cr - rs + r0 + sub, :]
        conv = conv + cw_ref[2:3, :] * a
        act_ref[r0:r0 + sub, :] = (jax.nn.gelu(conv) * g).astype(act_ref.dtype)


def _ffn_up_kernel(x_ref, xs_ref, wa_ref, wg_ref, wd_ref, cw_ref, cb_ref, init_s_ref,
                   act_ref, act_s_ref, tail_ref, tail_s_ref, wab_ref, wgb_ref, wdb_ref, ext_ref, ext_s_ref, *,
                   tm, sub, cr, tiles_per_seq, ms, cr_s, rs_s):
    i = pl.program_id(1)
    t_in_seq = i % tiles_per_seq
    tile = functools.partial(_ffn_rows, wa_ref=wab_ref, wg_ref=wgb_ref, cw_ref=cw_ref, cb_ref=cb_ref)

    @pl.when(i == 0)
    def _():
        wab_ref[...] = wa_ref[...].astype(BF16)
        wgb_ref[...] = wg_ref[...].astype(BF16)
        wdb_ref[...] = wd_ref[...].astype(BF16)
        ext_s_ref[0:cr_s, :] = init_s_ref[...]
        tile(xs_ref, ext_ref=ext_s_ref, act_ref=act_s_ref, rows=ms, sub=ms, cr=cr_s, rs=rs_s)
        tail_s_ref[...] = ext_s_ref[ms:ms + cr_s, :]

    @pl.when(t_in_seq == 0)
    def _():
        ext_ref[0:cr, :] = jnp.zeros((cr, ext_ref.shape[1]), F32)

    @pl.when(t_in_seq != 0)
    def _():
        ext_ref[0:cr, :] = ext_ref[tm:tm + cr, :]

    tile(x_ref, ext_ref=ext_ref, act_ref=act_ref, rows=tm, sub=sub, cr=cr, rs=1)

    @pl.when(t_in_seq == tiles_per_seq - 1)
    def _():
        tail_ref[0] = ext_ref[tm:tm + cr, :]


def ffn_up(h, hs, wa, wg, wd, li, conv_w, conv_b, init_s, rs_s, tm, sub, tf, rows_per_seq):
    m, ms = h.shape[0], hs.shape[0]
    cr = SUBLANES
    cr_s = init_s.shape[0]
    tiles_per_seq = rows_per_seq // tm
    nseq = m // rows_per_seq
    n_tiles = D_FF // tf
    slab = D_FF // n_tiles
    col = lambda f, i: (0, f)
    outs = pl.pallas_call(
        functools.partial(_ffn_up_kernel, tm=tm, sub=sub, cr=cr, tiles_per_seq=tiles_per_seq, ms=ms, cr_s=cr_s,
                          rs_s=rs_s),
        grid=(n_tiles, m // tm),
        in_specs=[
            pl.BlockSpec((tm, D_MODEL), lambda f, i: (i, 0)),
            pl.BlockSpec((ms, D_MODEL), lambda f, i: (0, 0)),
            pl.BlockSpec((None, D_MODEL, tf), lambda f, i: (li, 0, f)),
            pl.BlockSpec((None, D_MODEL, tf), lambda f, i: (li, 0, f)),
            pl.BlockSpec((None, slab, D_MODEL), lambda f, i: (li, f, 0), pipeline_mode=pl.Buffered(1)),
            pl.BlockSpec((None, CONV_W, tf), lambda f, i: (li, 0, f)),
            pl.BlockSpec((None, 1, tf), lambda f, i: (li, 0, f)),
            pl.BlockSpec((cr_s, tf), col),
        ],
        out_specs=[pl.BlockSpec((tm, tf), lambda f, i: (i, f)),
                   pl.BlockSpec((ms, tf), col),
                   pl.BlockSpec((1, cr, tf), lambda f, i: (i // tiles_per_seq, 0, f)),
                   pl.BlockSpec((cr_s, tf), col),
                   pl.BlockSpec((D_MODEL, tf), col),
                   pl.BlockSpec((D_MODEL, tf), col),
                   pl.BlockSpec((slab, D_MODEL), lambda f, i: (f, 0))],
        out_shape=[jax.ShapeDtypeStruct((m, D_FF), BF16), jax.ShapeDtypeStruct((ms, D_FF), BF16),
                   jax.ShapeDtypeStruct((nseq, cr, D_FF), F32), jax.ShapeDtypeStruct((cr_s, D_FF), F32),
                   jax.ShapeDtypeStruct((D_MODEL, D_FF), BF16), jax.ShapeDtypeStruct((D_MODEL, D_FF), BF16),
                   jax.ShapeDtypeStruct((D_FF, D_MODEL), BF16)],
        scratch_shapes=[pltpu.VMEM((cr + tm, tf), F32), pltpu.VMEM((cr_s + ms, tf), F32)],
        compiler_params=_params(("arbitrary", "arbitrary")),
        name="ffn_up",
    )(h, hs, wa, wg, wd, conv_w, conv_b.reshape(DEPTH, 1, D_FF), init_s)
    act, act_s, tail, tail_s, _, _, wd_b = outs
    return act, act_s, tail, tail_s, wd_b


def _s5_prep_kernel(lre_ref, lim_ref, step_ref, lre_x_ref, lim_x_ref, step_x_ref, bre_ref, bim_ref,
                    lbr_ref, lbi_ref, bbr_ref, bbi_ref):
    def lam_bar(lre, lim, log_step):
        delta = jnp.exp(log_step)
        mag = jnp.exp(lre * delta)
        return mag * jnp.cos(lim * delta), mag * jnp.sin(lim * delta)

    lbr, lbi = lam_bar(lre_ref[...], lim_ref[...], step_ref[...])
    lbr_ref[...] = lbr
    lbi_ref[...] = lbi
    lre, lim = lre_x_ref[...], lim_x_ref[...]
    xr, xi = lam_bar(lre, lim, step_x_ref[...])
    nr, ni = xr - 1.0, xi
    den = lre * lre + lim * lim
    cr = (nr * lre + ni * lim) / den
    ci = (ni * lre - nr * lim) / den
    br, bi = bre_ref[...], bim_ref[...]
    bbr_ref[...] = cr * br - ci * bi
    bbi_ref[...] = cr * bi + ci * br


def s5_prep(lam_re, lam_im, log_step, b_re, b_im):
    gp = (SSM_GROUPS, SSM_P)
    gph = (SSM_GROUPS, SSM_P * SSM_GROUP_CH)
    step = jnp.broadcast_to(log_step[:, None], gp)
    rep = lambda a: jnp.repeat(a, SSM_GROUP_CH, axis=1)
    full = lambda shp: pl.BlockSpec(shp, lambda: (0, 0))
    return pl.pallas_call(
        _s5_prep_kernel,
        in_specs=[full(gp)] * 3 + [full(gph)] * 5,
        out_specs=[full(gp), full(gp), full(gph), full(gph)],
        out_shape=[jax.ShapeDtypeStruct(gp, F32)] * 2 + [jax.ShapeDtypeStruct(gph, F32)] * 2,
        name="s5_prep",
    )(lam_re, lam_im, step, rep(lam_re), rep(lam_im), rep(step), b_re.reshape(gph), b_im.reshape(gph))


def _s5_dense_kernel(x_ref, gpre_ref, d_ref, lbr_ref, lbi_ref, bdr_ref, bdi_ref, cdr_ref, cdi_ref, s0r_ref, s0i_ref,
                     z_ref, sr_ref, si_ref, ur_ref, ui_ref, *, nb, tt):
    h = _rms(x_ref[...], gpre_ref[...])
    hb = h.astype(BF16)
    for kb in range(SSM_BLOCKS):
        ch = slice(kb * SSM_CH_BLOCK, (kb + 1) * SSM_CH_BLOCK)
        ln = slice(kb * SSM_LANE_BLOCK, (kb + 1) * SSM_LANE_BLOCK)
        ur_ref[...] = jnp.dot(hb[:, ch], bdr_ref[kb], preferred_element_type=F32)
        ui_ref[...] = jnp.dot(hb[:, ch], bdi_ref[kb], preferred_element_type=F32)
        a_re = jnp.broadcast_to(lbr_ref[:, ln], (nb, SSM_LANE_BLOCK))
        a_im = jnp.broadcast_to(lbi_ref[:, ln], (nb, SSM_LANE_BLOCK))
        s_re, s_im = s0r_ref[:, ln], s0i_ref[:, ln]
        for t in range(tt):
            rows = slice(t * nb, (t + 1) * nb)
            s_re, s_im = ((a_re * s_re - a_im * s_im) + ur_ref[rows, :],
                          (a_re * s_im + a_im * s_re) + ui_ref[rows, :])
            ur_ref[rows, :] = s_re
            ui_ref[rows, :] = s_im
        sr_ref[:, ln] = s_re
        si_ref[:, ln] = s_im
        y = (jnp.dot(ur_ref[...].astype(BF16), cdr_ref[kb], preferred_element_type=F32)
             - jnp.dot(ui_ref[...].astype(BF16), cdi_ref[kb], preferred_element_type=F32))
        y = y + d_ref[:, ch] * h[:, ch]
        z_ref[:, ch] = jax.nn.gelu(y).astype(z_ref.dtype)


def s5_dense(x, g_pre, d_skip, lbr, lbi, bdr, bdi, cdr, cdi, s0r, s0i, nb, tt):
    rows = nb * tt
    full = lambda shp: pl.BlockSpec(shp, lambda: tuple(0 for _ in shp))
    bd = full((SSM_BLOCKS, SSM_CH_BLOCK, SSM_LANE_BLOCK))
    cd = full((SSM_BLOCKS, SSM_LANE_BLOCK, SSM_CH_BLOCK))
    return pl.pallas_call(
        functools.partial(_s5_dense_kernel, nb=nb, tt=tt),
        in_specs=[full((rows, D_MODEL)), full((1, D_MODEL)), full((1, D_MODEL)), full((1, SSM_STATE)),
                  full((1, SSM_STATE)), bd, bd, cd, cd, full((nb, SSM_STATE)), full((nb, SSM_STATE))],
        out_specs=[full((rows, D_MODEL)), full((nb, SSM_STATE)), full((nb, SSM_STATE))],
        out_shape=[jax.ShapeDtypeStruct((rows, D_MODEL), BF16),
                   jax.ShapeDtypeStruct((nb, SSM_STATE), F32), jax.ShapeDtypeStruct((nb, SSM_STATE), F32)],
        scratch_shapes=[pltpu.VMEM((rows, SSM_LANE_BLOCK), F32), pltpu.VMEM((rows, SSM_LANE_BLOCK), F32)],
        compiler_params=pltpu.CompilerParams(vmem_limit_bytes=VMEM_LIMIT),
        name="s5_dense",
    )(x, g_pre.reshape(1, D_MODEL), d_skip.reshape(1, D_MODEL), lbr.reshape(1, SSM_STATE), lbi.reshape(1, SSM_STATE),
      bdr, bdi, cdr, cdi, s0r, s0i)


def _s5_seq_kernel(x_ref, gpre_ref, d_ref, lam_r_ref, lam_i_ref, bdr_ref, bdi_ref, cdr_ref, cdi_ref,
                   z_ref, sr_ref, si_ref, ur_ref, ui_ref, hs_ref, yo_ref):
    tt, pitch = S5_TT, S5_PITCH
    ti = pl.program_id(1)

    @pl.when(ti == 0)
    def _():
        sr_ref[...] = jnp.zeros(sr_ref.shape, F32)
        si_ref[...] = jnp.zeros(si_ref.shape, F32)

    h = _rms(x_ref[...], gpre_ref[...])
    zeros8 = jnp.zeros((SUBLANES, D_MODEL), F32)
    hs_ref[0:SUBLANES, :] = zeros8
    hs_ref[SUBLANES + tt:2 * SUBLANES + tt, :] = zeros8
    hs_ref[SUBLANES:SUBLANES + tt, :] = h
    hb = h.astype(BF16)
    hb_shift = hs_ref[4:tt + 12, :].astype(BF16)

    def even_rows(kb, lt):
        return pl.ds((kb * SUBLANES + 2 * lt) * pitch, tt)

    def odd_rows(kb, lt):
        return pl.ds((kb * SUBLANES + 2 * lt + 1) * pitch - 4, tt + SUBLANES)

    for kb in range(SSM_BLOCKS):
        ch = slice(kb * SSM_CH_BLOCK, (kb + 1) * SSM_CH_BLOCK)
        for u_ref, bd_ref in ((ur_ref, bdr_ref), (ui_ref, bdi_ref)):
            ue = jnp.dot(hb[:, ch], bd_ref[kb, :, 0:SSM_HALF], preferred_element_type=F32)
            uo = jnp.dot(hb_shift[:, ch], bd_ref[kb, :, SSM_HALF:SSM_LANE_BLOCK], preferred_element_type=F32)
            for lt in range(4):
                u_ref[even_rows(kb, lt), :] = ue[:, lt * LANES:(lt + 1) * LANES]
                u_ref[odd_rows(kb, lt), :] = uo[:, lt * LANES:(lt + 1) * LANES]

    blk = lambda q: slice(q * SUBLANES, (q + 1) * SUBLANES)
    a_re = [lam_r_ref[blk(q), :] for q in range(SSM_BLOCKS)]
    a_im = [lam_i_ref[blk(q), :] for q in range(SSM_BLOCKS)]

    def step(t, carry):
        new = []
        for q in range(SSM_BLOCKS):
            s_re, s_im = carry[2 * q], carry[2 * q + 1]
            rows = pl.ds(q * SUBLANES * pitch + t, SUBLANES, stride=pitch)
            n_re = (a_re[q] * s_re - a_im[q] * s_im) + ur_ref[rows, :]
            n_im = (a_re[q] * s_im + a_im[q] * s_re) + ui_ref[rows, :]
            ur_ref[rows, :] = n_re
            ui_ref[rows, :] = n_im
            new += [n_re, n_im]
        return tuple(new)

    init = []
    for q in range(SSM_BLOCKS):
        init += [sr_ref[0, blk(q), :], si_ref[0, blk(q), :]]
    final = lax.fori_loop(0, tt, step, tuple(init), unroll=2)
    for q in range(SSM_BLOCKS):
        sr_ref[0, blk(q), :] = final[2 * q]
        si_ref[0, blk(q), :] = final[2 * q + 1]

    for kb in range(SSM_BLOCKS):
        ch = slice(kb * SSM_CH_BLOCK, (kb + 1) * SSM_CH_BLOCK)

        def gather(u_ref, rows_of):
            return jnp.concatenate([u_ref[rows_of(kb, lt), :] for lt in range(4)], axis=1).astype(BF16)

        y_even = (jnp.dot(gather(ur_ref, even_rows), cdr_ref[kb, 0:SSM_HALF, :], preferred_element_type=F32)
                  - jnp.dot(gather(ui_ref, even_rows), cdi_ref[kb, 0:SSM_HALF, :], preferred_element_type=F32))
        yo_ref[...] = (jnp.dot(gather(ur_ref, odd_rows), cdr_ref[kb, SSM_HALF:SSM_LANE_BLOCK, :],
                               preferred_element_type=F32)
                       - jnp.dot(gather(ui_ref, odd_rows), cdi_ref[kb, SSM_HALF:SSM_LANE_BLOCK, :],
                                 preferred_element_type=F32))
        y = y_even + yo_ref[4:tt + 4, :]
        y = y + d_ref[:, ch] * h[:, ch]
        z_ref[:, ch] = jax.nn.gelu(y).astype(z_ref.dtype)


def s5_seq(x, g_pre, d_skip, lam_r, lam_i, bdr, bdi, cdr, cdi, nseq):
    m = x.shape[0]
    ntt = m // (nseq * S5_TT)
    fixed2 = lambda shp: pl.BlockSpec(shp, lambda si, ti: (0, 0))
    state = pl.BlockSpec((1, SSM_SLABS, LANES), lambda si, ti: (si, 0, 0))
    bd = _resident((SSM_BLOCKS, SSM_CH_BLOCK, SSM_LANE_BLOCK), lambda si, ti: (0, 0, 0))
    cd = _resident((SSM_BLOCKS, SSM_LANE_BLOCK, SSM_CH_BLOCK), lambda si, ti: (0, 0, 0))
    return pl.pallas_call(
        _s5_seq_kernel,
        grid=(nseq, ntt),
        in_specs=[pl.BlockSpec((S5_TT, D_MODEL), lambda si, ti: (si * ntt + ti, 0)),
                  fixed2((1, D_MODEL)), fixed2((1, D_MODEL)), fixed2((SSM_SLABS, LANES)), fixed2((SSM_SLABS, LANES)),
                  bd, bd, cd, cd],
        out_specs=[pl.BlockSpec((S5_TT, D_MODEL), lambda si, ti: (si * ntt + ti, 0)), state, state],
        out_shape=[jax.ShapeDtypeStruct((m, D_MODEL), BF16),
                   jax.ShapeDtypeStruct((nseq, SSM_SLABS, LANES), F32),
                   jax.ShapeDtypeStruct((nseq, SSM_SLABS, LANES), F32)],
        scratch_shapes=[pltpu.VMEM((SSM_SLABS * S5_PITCH, LANES), F32), pltpu.VMEM((SSM_SLABS * S5_PITCH, LANES), F32),
                        pltpu.VMEM((S5_TT + 2 * SUBLANES, D_MODEL), F32),
                        pltpu.VMEM((S5_TT + SUBLANES, SSM_CH_BLOCK), F32)],
        compiler_params=_params(("parallel", "arbitrary")),
        name="s5_seq",
    )(x, g_pre.reshape(1, D_MODEL), d_skip.reshape(1, D_MODEL), lam_r, lam_i, bdr, bdi, cdr, cdi)


def _to_slabs(v):
    lead = v.shape[:-1]
    v = v.reshape(lead + (SSM_BLOCKS, SUBLANES, LANES))
    return jnp.take(v, jnp.array(SLAB_LANE_TILE), axis=-2).reshape(lead + (SSM_SLABS, LANES))


def _from_slabs(s):
    lead = s.shape[:-2]
    s = s.reshape(lead + (SSM_BLOCKS, SUBLANES, LANES))
    return jnp.take(s, jnp.array(LANE_TILE_SLAB), axis=-2).reshape(lead + (SSM_STATE,))


def _block_diag(w):
    gpb = SSM_GROUPS // SSM_BLOCKS
    a, b = w.shape[1], w.shape[2]
    eye = jnp.eye(gpb, dtype=w.dtype)
    out = jnp.einsum('kgab,gf->kgafb', w.reshape(SSM_BLOCKS, gpb, a, b), eye)
    return out.reshape(SSM_BLOCKS, gpb * a, gpb * b)


def _slopes():
    return 2.0 ** (-8.0 * jnp.arange(1, SWA_Q_HEADS + 1, dtype=F32) / SWA_Q_HEADS)


def kernel(x_prompt, x_sample, state_ret, cache_swa_k, cache_swa_v, state_ssm_re, state_ssm_im, state_ffn_conv, norm_mix_pre, norm_mix_post, norm_ffn_pre, norm_ffn_post, w_in_even, w_out_even, swa_sinks, ssm_lam_re, ssm_lam_im, ssm_log_step, ssm_b_re, ssm_b_im, ssm_c_re, ssm_c_im, ssm_d, w_glu, ffn_w_a, ffn_w_g, ffn_conv_w, ffn_conv_b, ffn_w_down):
    pb, pl_len, _ = x_prompt.shape
    sb, sl_len, _ = x_sample.shape
    mp = pb * pl_len
    ms = sb * sl_len
    s_pad = BF16_ROWS
    carry_rows_s = (CONV_W - 1) * sb
    slopes = _slopes()

    xp = x_prompt.reshape(mp, D_MODEL)
    xs = x_sample.transpose(1, 0, 2).reshape(ms, D_MODEL)

    hp = rmsnorm_bf16(xp, norm_mix_pre[0], 512)
    hs = rmsnorm_bf16(xs, norm_mix_pre[0], ms)

    ret_p, ret_s, wk_p, wk_s, wv_p, wv_s = [], [], [], [], [], []
    sre_p, sre_s, sim_p, sim_s, conv_p, conv_s = [], [], [], [], [], []
    w_glu_b = None

    for layer in range(DEPTH):
        i = layer // 2
        if layer % 2 == 0:
            proj, proj_s, w_out_b = in_proj(hp, hs, w_in_even, w_out_even, i, 1024, 768)
            proj = proj.reshape(pb, pl_len, EVEN_IN)
            o_ret, s_ret = retention(proj, jnp.zeros((pb, RET_HEADS, RET_DK, RET_DV), F32), RET_CHUNK, RET_CHUNK)
            kcol, vcol = KV_COL // LANES, KV_COL // LANES + 1
            blk = lambda c, prev: pl.BlockSpec(
                (1, SWA_BLOCK, LANES),
                (lambda bi, qi: (bi, jnp.maximum(qi - 1, 0), c)) if prev else (lambda bi, qi: (bi, qi, c)))
            o_swa = swa_attention(
                slopes, swa_sinks[i], proj,
                pl.BlockSpec((1, SWA_BLOCK, SWA_WIDTH), lambda bi, qi: (bi, qi, Q_COL // SWA_WIDTH)),
                [proj, proj, proj, proj], [blk(kcol, True), blk(vcol, True), blk(kcol, False), blk(vcol, False)],
                pb, pl_len // SWA_BLOCK, SWA_BLOCK, 0, SWA_BLOCK)
            ret_p.append(s_ret)
            kv_shape = (pb, WINDOW, SWA_KV_HEADS, SWA_HD)
            wk_p.append(proj[:, pl_len - WINDOW:, KV_COL:KV_COL + KV_WIDTH].reshape(kv_shape))
            wv_p.append(proj[:, pl_len - WINDOW:, KV_COL + KV_WIDTH:KV_COL + 2 * KV_WIDTH].reshape(kv_shape))

            proj_bt = proj_s.reshape(sl_len, sb, EVEN_IN).transpose(1, 0, 2)
            proj_pad = jnp.pad(proj_bt, ((0, 0), (0, s_pad - sl_len), (0, 0)))
            o_ret_s, s_ret_s = retention(proj_pad, state_ret[i], sl_len, s_pad)
            win = cache_swa_k.shape[2]
            ck = cache_swa_k[i].reshape(sb, win, KV_WIDTH)
            cv = cache_swa_v[i].reshape(sb, win, KV_WIDTH)
            cache_spec = pl.BlockSpec((1, win, KV_WIDTH), lambda bi, qi: (bi, 0, 0))
            cur = lambda c: pl.BlockSpec((1, s_pad, LANES), lambda bi, qi: (bi, 0, c))
            o_swa_s = swa_attention(
                slopes, swa_sinks[i], proj_pad,
                pl.BlockSpec((1, s_pad, SWA_WIDTH), lambda bi, qi: (bi, 0, Q_COL // SWA_WIDTH)),
                [ck, cv, proj_pad, proj_pad], [cache_spec, cache_spec, cur(kcol), cur(vcol)],
                sb, 1, s_pad, win, 0)
            ret_s.append(s_ret_s)
            k_new = proj_bt[:, :, KV_COL:KV_COL + KV_WIDTH]
            v_new = proj_bt[:, :, KV_COL + KV_WIDTH:KV_COL + 2 * KV_WIDTH]
            kv_shape_s = (sb, win, SWA_KV_HEADS, SWA_HD)
            wk_s.append(jnp.concatenate([ck, k_new], axis=1)[:, sl_len:].reshape(kv_shape_s))
            wv_s.append(jnp.concatenate([cv, v_new], axis=1)[:, sl_len:].reshape(kv_shape_s))
            tb = lambda o: o[:, :sl_len].transpose(1, 0, 2).reshape(ms, -1)

            xp, xs, hp, hs, _ = proj_residual(
                [o_ret.reshape(mp, RET_WIDTH), o_swa.reshape(mp, SWA_WIDTH)], [tb(o_ret_s), tb(o_swa_s)],
                w_out_b, [0, 1], False, xp, xs, norm_mix_post[layer], norm_ffn_pre[layer], 512, "out_proj")
        else:
            lbr, lbi, bbr, bbi = s5_prep(ssm_lam_re[i], ssm_lam_im[i], ssm_log_step[i], ssm_b_re[i], ssm_b_im[i])
            to_hp = lambda a: a.reshape(SSM_GROUPS, SSM_P, SSM_GROUP_CH).transpose(0, 2, 1)
            bdr = _block_diag(to_hp(bbr)).astype(BF16)
            bdi = _block_diag(to_hp(bbi)).astype(BF16)
            cdr = _block_diag(ssm_c_re[i].transpose(0, 2, 1)).astype(BF16)
            cdi = _block_diag(ssm_c_im[i].transpose(0, 2, 1)).astype(BF16)
            lam_r = _to_slabs(lbr.reshape(SSM_STATE))
            lam_i = _to_slabs(lbi.reshape(SSM_STATE))
            zp, s_re, s_im = s5_seq(xp, norm_mix_pre[layer], ssm_d[i], lam_r, lam_i, bdr, bdi, cdr, cdi, pb)
            sre_p.append(_from_slabs(s_re).reshape(pb, SSM_GROUPS, SSM_P))
            sim_p.append(_from_slabs(s_im).reshape(pb, SSM_GROUPS, SSM_P))
            zs, s_re_s, s_im_s = s5_dense(xs, norm_mix_pre[layer], ssm_d[i], lbr, lbi, bdr, bdi, cdr, cdi,
                                          state_ssm_re[i].reshape(sb, SSM_STATE),
                                          state_ssm_im[i].reshape(sb, SSM_STATE), sb, sl_len)
            sre_s.append(s_re_s.reshape(sb, SSM_GROUPS, SSM_P))
            sim_s.append(s_im_s.reshape(sb, SSM_GROUPS, SSM_P))
            xp, xs, hp, hs, _ = proj_residual([zp], [zs], w_glu_b, [0], True, xp, xs, norm_mix_post[layer],
                                              norm_ffn_pre[layer], 256, "glu_proj")

        init_s = state_ffn_conv[layer].transpose(1, 0, 2).reshape(carry_rows_s, D_FF)
        act, act_s, tail, tail_s, w_down_b = ffn_up(hp, hs, ffn_w_a, ffn_w_g, ffn_w_down, layer, ffn_conv_w,
                                                    ffn_conv_b, init_s, sb, 1024, 512, 512, pl_len)
        conv_p.append(tail[:, SUBLANES - (CONV_W - 1):])
        conv_s.append(tail_s.reshape(CONV_W - 1, sb, D_FF).transpose(1, 0, 2))
        next_even = layer + 1 < DEPTH and (layer + 1) % 2 == 0
        next_odd = layer + 1 < DEPTH and (layer + 1) % 2 == 1
        xp, xs, hp, hs, w_cast = proj_residual(
            [act], [act_s], w_down_b, [0], False, xp, xs, norm_ffn_post[layer],
            norm_mix_pre[layer + 1] if next_even else None, 256, "ffn_down",
            cast=(w_glu, (layer + 1) // 2) if next_odd else None)
        if next_odd:
            w_glu_b = w_cast

    y_prompt = xp.reshape(pb, pl_len, D_MODEL)
    y_sample = xs.reshape(sl_len, sb, D_MODEL).transpose(1, 0, 2)
    return (y_prompt, y_sample, jnp.stack(ret_p), jnp.stack(ret_s), jnp.stack(wk_p), jnp.stack(wk_s),
            jnp.stack(wv_p), jnp.stack(wv_s), jnp.stack(sre_p), jnp.stack(sre_s), jnp.stack(sim_p),
            jnp.stack(sim_s), jnp.stack(conv_p), jnp.stack(conv_s))
```

```python
import functools

import jax
import jax.numpy as jnp
from jax import lax
from jax.experimental import pallas as pl
from jax.experimental.pallas import tpu as pltpu

F32 = jnp.float32
BF16 = jnp.bfloat16

D_MODEL = 2048
DEPTH = 4
RET_HEADS = 8
RET_DK = 128
RET_DV = 128
RET_CHUNK = 128
RET_WIDTH = RET_HEADS * RET_DV
SWA_Q_HEADS = 16
SWA_KV_HEADS = 2
SWA_HD = 64
WINDOW = 128
SWA_BLOCK = 128
SWA_WIDTH = SWA_Q_HEADS * SWA_HD
SWA_CHAINS = 4
RET_CHUNKS_PER_STEP = 4
SAMPLE_SEQS_PER_STEP = 4
EVEN_IN = 5376
Q_COL = RET_WIDTH * 4
KV_COL = Q_COL + SWA_WIDTH
KV_WIDTH = SWA_KV_HEADS * SWA_HD
SSM_GROUP_CH = 16
SSM_GROUPS = 128
SSM_P = 64
SSM_STATE = SSM_GROUPS * SSM_P
SSM_LANE_BLOCK = 1024
SSM_CH_BLOCK = 256
SSM_BLOCKS = SSM_STATE // SSM_LANE_BLOCK
SSM_HALF = SSM_LANE_BLOCK // 2
D_FF = 5632
CONV_W = 3
NORM_EPS = 1e-6

LANES = 128
SUBLANES = 8
BF16_ROWS = 16
VMEM_LIMIT = 56 * 1024 * 1024

SSM_SLABS = SSM_STATE // LANES
S5_TT = 256
S5_PITCH = S5_TT + 4
SLAB_LANE_TILE = (0, 4, 1, 5, 2, 6, 3, 7)
LANE_TILE_SLAB = (0, 2, 4, 6, 1, 3, 5, 7)


def _params(sem, vmem=VMEM_LIMIT):
    return pltpu.CompilerParams(dimension_semantics=sem, vmem_limit_bytes=vmem)


def _rms(x, g):
    return x * lax.rsqrt(jnp.mean(x * x, axis=-1, keepdims=True) + NORM_EPS) * g


def _resident(shape, index_map):
    return pl.BlockSpec(shape, index_map, pipeline_mode=pl.Buffered(1))


def _norm_kernel(x_ref, g_ref, o_ref):
    o_ref[...] = _rms(x_ref[...], g_ref[...]).astype(o_ref.dtype)


def rmsnorm_bf16(x, g, tm):
    m, d = x.shape
    return pl.pallas_call(
        _norm_kernel,
        grid=(m // tm,),
        in_specs=[pl.BlockSpec((tm, d), lambda i: (i, 0)), pl.BlockSpec((1, d), lambda i: (0, 0))],
        out_specs=pl.BlockSpec((tm, d), lambda i: (i, 0)),
        out_shape=jax.ShapeDtypeStruct((m, d), BF16),
        compiler_params=_params(("parallel",)),
        name="rmsnorm",
    )(x, g.reshape(1, d))


def _in_proj_kernel(x_ref, xs_ref, w_ref, wo_ref, o_ref, os_ref, wob_ref, wb_ref):
    @pl.when(pl.program_id(1) == 0)
    def _():
        wb_ref[...] = w_ref[...].astype(BF16)
        wob_ref[...] = wo_ref[...].astype(BF16)
        os_ref[...] = jnp.dot(xs_ref[...], wb_ref[...], preferred_element_type=F32)

    o_ref[...] = jnp.dot(x_ref[...], wb_ref[...], preferred_element_type=F32)


def in_proj(h, hs, w_in, w_out, li, tm, tn):
    m, ms = h.shape[0], hs.shape[0]
    n_tiles = EVEN_IN // tn
    slab = 512
    n_slabs = D_MODEL // slab
    assert n_slabs <= n_tiles
    slab_idx = lambda j, i: (li, jnp.minimum(j, n_slabs - 1), 0)
    return pl.pallas_call(
        _in_proj_kernel,
        grid=(n_tiles, m // tm),
        in_specs=[pl.BlockSpec((tm, D_MODEL), lambda j, i: (i, 0)),
                  pl.BlockSpec((ms, D_MODEL), lambda j, i: (0, 0)),
                  pl.BlockSpec((None, D_MODEL, tn), lambda j, i: (li, 0, j)),
                  pl.BlockSpec((None, slab, D_MODEL), slab_idx, pipeline_mode=pl.Buffered(1))],
        out_specs=[pl.BlockSpec((tm, tn), lambda j, i: (i, j)),
                   pl.BlockSpec((ms, tn), lambda j, i: (0, j)),
                   pl.BlockSpec((slab, D_MODEL), lambda j, i: (jnp.minimum(j, n_slabs - 1), 0))],
        out_shape=[jax.ShapeDtypeStruct((m, EVEN_IN), F32), jax.ShapeDtypeStruct((ms, EVEN_IN), F32),
                   jax.ShapeDtypeStruct((D_MODEL, D_MODEL), BF16)],
        scratch_shapes=[pltpu.VMEM((D_MODEL, tn), BF16)],
        compiler_params=_params(("arbitrary", "arbitrary")),
        name="in_proj",
    )(h, hs, w_in, w_out)


def _retention_kernel(decay_ref, q_ref, k_ref, v_ref, g_ref, intra_ref, read_ref, write_ref, s0_ref,
                      o_ref, s_ref, *, nseq, chunks, c):
    ci = pl.program_id(1)

    @pl.when(ci == 0)
    def _():
        s_ref[...] = s0_ref[...]

    heads = [slice(h * RET_DK, (h + 1) * RET_DK) for h in range(RET_HEADS)]
    nt = (((1,), (1,)), ((), ()))
    tn = (((0,), (0,)), ((), ()))
    jobs = [(b, h) for b in range(nseq) for h in range(RET_HEADS)]
    for cc in range(chunks):
        rows = slice(cc * c, (cc + 1) * c)
        first = []
        for b, h in jobs:
            sl = heads[h]
            q = q_ref[b, rows, sl].astype(BF16)
            k = k_ref[b, rows, sl] * (RET_DK ** -0.5)
            v = v_ref[b, rows, sl].astype(BF16)
            s = s_ref[b, h]
            sc = lax.dot_general(q, k.astype(BF16), nt, preferred_element_type=F32)
            qs = jnp.dot(q, s.astype(BF16), preferred_element_type=F32)
            kv = lax.dot_general((k * write_ref[h]).astype(BF16), v, tn, preferred_element_type=F32)
            first.append((v, s, sc, qs, kv))
        outs = []
        for (b, h), (v, s, sc, qs, kv) in zip(jobs, first):
            outs.append(jnp.dot((sc * intra_ref[h]).astype(BF16), v, preferred_element_type=F32) + qs * read_ref[h])
            s_ref[b, h] = s * decay_ref[h] + kv
        for (b, h), o in zip(jobs, outs):
            g = g_ref[b, rows, heads[h]]
            o = o * lax.rsqrt(jnp.mean(o * o, axis=-1, keepdims=True) + NORM_EPS)
            o_ref[b, rows, heads[h]] = (o * (g * jax.nn.sigmoid(g))).astype(o_ref.dtype)


def _retention_tables(c_real, c_pad):
    lg = jnp.log(1.0 - 2.0 ** (-5.0 - jnp.arange(RET_HEADS, dtype=F32)))
    idx = jnp.arange(c_pad, dtype=F32)
    diff = idx[:, None] - idx[None, :]
    intra = jnp.where(diff >= 0, jnp.exp(lg[:, None, None] * jnp.maximum(diff, 0.0)), 0.0)
    read = jnp.exp(lg[:, None] * (idx[None, :] + 1.0))
    write = jnp.exp(lg[:, None] * (c_real - 1.0 - idx[None, :]))
    decay = jnp.exp(lg * c_real)
    bshape = (RET_HEADS, c_pad, RET_DV)
    return decay, intra, jnp.broadcast_to(read[:, :, None], bshape), jnp.broadcast_to(write[:, :, None], bshape)


def retention(proj, s0, c_real, c_pad, nseq, chunks):
    b, l, _ = proj.shape
    rows = chunks * c_pad
    decay, intra, read, write = _retention_tables(c_real, c_pad)

    def col(j):
        return pl.BlockSpec((nseq, rows, RET_WIDTH), lambda bi, ci: (bi, ci, j))

    state_spec = pl.BlockSpec((nseq, RET_HEADS, RET_DK, RET_DV), lambda bi, ci: (bi, 0, 0, 0))
    return pl.pallas_call(
        functools.partial(_retention_kernel, nseq=nseq, chunks=chunks, c=c_pad),
        grid=(b // nseq, l // rows),
        in_specs=[
            pl.BlockSpec(memory_space=pltpu.SMEM),
            col(0), col(1), col(2), col(3),
            pl.BlockSpec((RET_HEADS, c_pad, c_pad), lambda bi, ci: (0, 0, 0)),
            pl.BlockSpec((RET_HEADS, c_pad, RET_DV), lambda bi, ci: (0, 0, 0)),
            pl.BlockSpec((RET_HEADS, c_pad, RET_DV), lambda bi, ci: (0, 0, 0)),
            state_spec,
        ],
        out_specs=[pl.BlockSpec((nseq, rows, RET_WIDTH), lambda bi, ci: (bi, ci, 0)), state_spec],
        out_shape=[jax.ShapeDtypeStruct((b, l, RET_WIDTH), BF16),
                   jax.ShapeDtypeStruct((b, RET_HEADS, RET_DK, RET_DV), F32)],
        compiler_params=_params(("parallel", "arbitrary")),
        name="retention",
    )(decay, proj, proj, proj, proj, intra, read, write, s0)


def _swa_kernel(bias_p_ref, bias_c_ref, sink_ref, q_ref, kp_ref, vp_ref, kc_ref, vc_ref, o_ref, *,
                nseq, group, tq, q0_base, q0_step):
    prev_always_visible = q0_step == 0 and q0_base >= WINDOW
    if not prev_always_visible:
        q0 = q0_base + pl.program_id(1) * q0_step
        prev_penalty = jnp.where(q0 >= WINDOW, 0.0, -jnp.inf).astype(F32)
    lane_lo = lax.broadcasted_iota(jnp.int32, (tq, LANES), 1) < SWA_HD
    scale = SWA_HD ** -0.5
    nt = (((1,), (1,)), ((), ()))
    pairs_per_kv = SWA_Q_HEADS // SWA_KV_HEADS // 2
    lane_sum = lambda x: jnp.sum(x, axis=-1, keepdims=True)
    lane_max = lambda x: jnp.max(x, axis=-1, keepdims=True)

    def both(ref, b):
        x = ref[b]
        return x.astype(BF16), pltpu.roll(x, SWA_HD, 1).astype(BF16)

    keys = [(both(kp_ref, b), both(kc_ref, b)) for b in range(nseq)]
    vals = [(both(vp_ref, b), both(vc_ref, b)) for b in range(nseq)]
    all_chains = [(b, kv, parity) for b in range(nseq) for kv in range(SWA_KV_HEADS) for parity in range(2)]
    outs = {}
    for g0 in range(0, len(all_chains), group):
        chains = all_chains[g0:g0 + group]
        scores = []
        for b, kv, parity in chains:
            swapped = (kv == 0) != (parity == 0)
            (kp, kp_sw), (kc, kc_sw) = keys[b]
            kprev, kcur = (kp_sw, kc_sw) if swapped else (kp, kc)
            qs = []
            for p in range(pairs_per_kv):
                pair = kv * pairs_per_kv + p
                qp = q_ref[b, :, pair * LANES:(pair + 1) * LANES]
                qs.append(jnp.where(lane_lo, qp, 0.0) if parity == 0 else jnp.where(lane_lo, 0.0, qp))
            qh = jnp.concatenate(qs, axis=0).astype(BF16)
            scores.append((lax.dot_general(qh, kprev, nt, preferred_element_type=F32),
                           lax.dot_general(qh, kcur, nt, preferred_element_type=F32)))
        probs = []
        for (b, kv, parity), (sp, sc) in zip(chains, scores):
            chain = 2 * kv + parity
            sink = sink_ref[chain][:, :1]
            sp = sp * scale + bias_p_ref[chain]
            sc = sc * scale + bias_c_ref[chain]
            if not prev_always_visible:
                sp = sp + prev_penalty
            if tq == WINDOW:
                m = jnp.maximum(lane_max(jnp.maximum(sp, sc)), sink)
            else:
                m = jnp.maximum(jnp.maximum(lane_max(sp), lane_max(sc)), sink)
            pp = jnp.exp(sp - m)
            pc = jnp.exp(sc - m)
            psum = lane_sum(pp + pc) if tq == WINDOW else lane_sum(pp) + lane_sum(pc)
            denom = psum + jnp.exp(sink - m)
            probs.append(((pp / denom).astype(BF16), (pc / denom).astype(BF16)))
        for (b, kv, parity), (pp, pc) in zip(chains, probs):
            swapped = (kv == 0) != (parity == 0)
            (vp, vp_sw), (vc, vc_sw) = vals[b]
            vprev, vcur = (vp_sw, vc_sw) if swapped else (vp, vc)
            outs[b, kv, parity] = (jnp.dot(pp, vprev, preferred_element_type=F32)
                                   + jnp.dot(pc, vcur, preferred_element_type=F32))
        for b, kv, parity in chains:
            if parity == 0:
                continue
            for p in range(pairs_per_kv):
                pair = kv * pairs_per_kv + p
                rs = slice(p * tq, (p + 1) * tq)
                o_ref[b, :, pair * LANES:(pair + 1) * LANES] = jnp.where(lane_lo, outs[b, kv, 0][rs],
                                                                          outs[b, kv, 1][rs]).astype(o_ref.dtype)


def _swa_tables(slopes, sinks, tq):
    per = SWA_Q_HEADS // SWA_CHAINS

    def rows_of(vals):
        v = vals.astype(F32).reshape(SWA_KV_HEADS, per, 2).transpose(0, 2, 1).reshape(SWA_CHAINS, per)
        return jnp.broadcast_to(v[:, :, None, None], (SWA_CHAINS, per, tq, 1))

    r = jnp.arange(tq, dtype=jnp.int32)[:, None]
    dist_p = r - jnp.arange(WINDOW, dtype=jnp.int32)[None, :] + WINDOW
    dist_c = r - jnp.arange(tq, dtype=jnp.int32)[None, :]
    slope = rows_of(slopes)
    bias_p = jnp.where(dist_p <= WINDOW, -(slope * dist_p.astype(F32)), -jnp.inf)
    bias_c = jnp.where(dist_c >= 0, -(slope * dist_c.astype(F32)), -jnp.inf)
    sink = jnp.broadcast_to(rows_of(sinks), (SWA_CHAINS, per, tq, LANES))
    flat = lambda t: t.reshape(SWA_CHAINS, per * tq, t.shape[-1])
    return flat(bias_p), flat(bias_c), flat(sink)


def swa_attention(slopes, sinks, q_arr, q_spec, kv_arrs, kv_specs, b, nseq, group, nb, tq, q0_base, q0_step):
    assert q0_base % WINDOW == 0 and q0_step % WINDOW == 0
    rows = (SWA_Q_HEADS // SWA_CHAINS) * tq
    table = lambda width: pl.BlockSpec((SWA_CHAINS, rows, width), lambda bi, i: (0, 0, 0))
    return pl.pallas_call(
        functools.partial(_swa_kernel, nseq=nseq, group=group, tq=tq, q0_base=q0_base, q0_step=q0_step),
        grid=(b // nseq, nb),
        in_specs=[table(WINDOW), table(tq), table(LANES), q_spec] + kv_specs,
        out_specs=pl.BlockSpec((nseq, tq, SWA_WIDTH), lambda bi, i: (bi, i, 0)),
        out_shape=jax.ShapeDtypeStruct((b, nb * tq, SWA_WIDTH), BF16),
        compiler_params=_params(("parallel", "parallel")),
        name="swa",
    )(*_swa_tables(slopes, sinks, tq), q_arr, *kv_arrs)


def _res_tile(acts, w_main, w_gate, x_ref, gpost_ref, gnext_ref, xo_ref, ho_ref):
    def mm(ws):
        acc = jnp.dot(acts[0][...], ws[0][...], preferred_element_type=F32)
        for a, w in zip(acts[1:], ws[1:]):
            acc = acc + jnp.dot(a[...], w[...], preferred_element_type=F32)
        return acc

    y = mm(w_main)
    if w_gate:
        y = y * jax.nn.sigmoid(mm(w_gate))
    x_new = x_ref[...] + _rms(y, gpost_ref[...])
    xo_ref[...] = x_new
    if ho_ref is not None:
        ho_ref[...] = _rms(x_new, gnext_ref[...]).astype(ho_ref.dtype)


def _proj_res_kernel(*refs, n_pieces, glu, has_next, has_cast):
    it = iter(refs)
    take = lambda k: [next(it) for _ in range(k)]
    acts, acts_s, w_main = take(n_pieces), take(n_pieces), take(n_pieces)
    w_gate = take(n_pieces) if glu else []
    x_ref, xs_ref, gpost_ref = take(3)
    gnext_ref = next(it) if has_next else None
    src_ref = next(it) if has_cast else None
    xo_ref, xos_ref = take(2)
    ho_ref, hos_ref = take(2) if has_next else (None, None)
    dst_ref = next(it) if has_cast else None

    _res_tile(acts, w_main, w_gate, x_ref, gpost_ref, gnext_ref, xo_ref, ho_ref)

    @pl.when(pl.program_id(0) == 0)
    def _():
        _res_tile(acts_s, w_main, w_gate, xs_ref, gpost_ref, gnext_ref, xos_ref, hos_ref)

    if has_cast:
        dst_ref[...] = src_ref[...].astype(dst_ref.dtype)


def proj_residual(acts, acts_s, w, w_row_blocks, glu, x, xs, g_post, g_next, tm, name, cast=None):
    m, ms = x.shape[0], xs.shape[0]
    steps = m // tm
    n_pieces = len(acts)
    has_next = g_next is not None
    row = lambda i: (i, 0)
    fixed = lambda i: (0, 0)
    in_specs = [pl.BlockSpec((tm, a.shape[1]), row) for a in acts]
    in_specs += [pl.BlockSpec((ms, a.shape[1]), fixed) for a in acts_s]
    args = list(acts) + list(acts_s)
    for half in range(2 if glu else 1):
        for p in range(n_pieces):
            in_specs.append(_resident((acts[p].shape[1], D_MODEL),
                                      functools.partial(lambda i, rb, cb: (rb, cb), rb=w_row_blocks[p], cb=half)))
            args.append(w)
    in_specs += [pl.BlockSpec((tm, D_MODEL), row), pl.BlockSpec((ms, D_MODEL), fixed), pl.BlockSpec((1, D_MODEL), fixed)]
    args += [x, xs, g_post.reshape(1, D_MODEL)]
    if has_next:
        in_specs.append(pl.BlockSpec((1, D_MODEL), fixed))
        args.append(g_next.reshape(1, D_MODEL))
    if cast is not None:
        src, li = cast
        slab = src.shape[1] // steps
        in_specs.append(pl.BlockSpec((None, slab, src.shape[2]), lambda i: (li, i, 0)))
        args.append(src)
    out_specs = [pl.BlockSpec((tm, D_MODEL), row), pl.BlockSpec((ms, D_MODEL), fixed)]
    out_shape = [jax.ShapeDtypeStruct((m, D_MODEL), F32), jax.ShapeDtypeStruct((ms, D_MODEL), F32)]
    if has_next:
        out_specs += [pl.BlockSpec((tm, D_MODEL), row), pl.BlockSpec((ms, D_MODEL), fixed)]
        out_shape += [jax.ShapeDtypeStruct((m, D_MODEL), BF16), jax.ShapeDtypeStruct((ms, D_MODEL), BF16)]
    if cast is not None:
        out_specs.append(pl.BlockSpec((slab, src.shape[2]), row))
        out_shape.append(jax.ShapeDtypeStruct(src.shape[1:], BF16))
    outs = pl.pallas_call(
        functools.partial(_proj_res_kernel, n_pieces=n_pieces, glu=glu, has_next=has_next, has_cast=cast is not None),
        grid=(steps,),
        in_specs=in_specs,
        out_specs=out_specs,
        out_shape=out_shape,
        compiler_params=_params(("arbitrary",)),
        name=name,
    )(*args)
    outs = list(outs)
    x_new, xs_new = outs[0], outs[1]
    h_new, hs_new = (outs[2], outs[3]) if has_next else (None, None)
    w_cast = outs[-1] if cast is not None else None
    return x_new, xs_new, h_new, hs_new, w_cast


def _ffn_rows(x_ref, wa_ref, wg_ref, cw_ref, cb_ref, carry_ref, act_ref, *, rs):
    x = x_ref[...]
    a = jnp.dot(x, wa_ref[...], preferred_element_type=F32)
    g = jnp.dot(x, wg_ref[...], preferred_element_type=F32)
    cr = carry_ref.shape[0]
    ext = jnp.concatenate([carry_ref[...], a], axis=0)
    conv = cb_ref[...] + cw_ref[0:1, :] * pltpu.roll(ext, 2 * rs, 0)[cr:]
    conv = conv + cw_ref[1:2, :] * pltpu.roll(ext, rs, 0)[cr:]
    conv = conv + cw_ref[2:3, :] * a
    act_ref[...] = (jax.nn.gelu(conv) * g).astype(act_ref.dtype)
    carry_ref[...] = ext[a.shape[0]:]


def _ffn_up_kernel(x_ref, xs_ref, wa_ref, wg_ref, wd_ref, cw_ref, cb_ref, init_s_ref,
                   act_ref, act_s_ref, tail_ref, tail_s_ref, wab_ref, wgb_ref, wdb_ref, carry_ref, carry_s_ref, *,
                   tiles_per_seq, rs_s):
    i = pl.program_id(1)
    t_in_seq = i % tiles_per_seq
    tile = functools.partial(_ffn_rows, wa_ref=wab_ref, wg_ref=wgb_ref, cw_ref=cw_ref, cb_ref=cb_ref)

    @pl.when(i == 0)
    def _():
        wab_ref[...] = wa_ref[...].astype(BF16)
        wgb_ref[...] = wg_ref[...].astype(BF16)
        wdb_ref[...] = wd_ref[...].astype(BF16)
        carry_s_ref[...] = init_s_ref[...]
        tile(xs_ref, carry_ref=carry_s_ref, act_ref=act_s_ref, rs=rs_s)
        tail_s_ref[...] = carry_s_ref[...]

    @pl.when(t_in_seq == 0)
    def _():
        carry_ref[...] = jnp.zeros(carry_ref.shape, F32)

    tile(x_ref, carry_ref=carry_ref, act_ref=act_ref, rs=1)

    @pl.when(t_in_seq == tiles_per_seq - 1)
    def _():
        tail_ref[0] = carry_ref[...]


def ffn_up(h, hs, wa, wg, wd, li, conv_w, conv_b, init_s, rs_s, tm, tf, rows_per_seq):
    m, ms = h.shape[0], hs.shape[0]
    cr = SUBLANES
    cr_s = init_s.shape[0]
    tiles_per_seq = rows_per_seq // tm
    nseq = m // rows_per_seq
    n_tiles = D_FF // tf
    slab = D_FF // n_tiles
    col = lambda f, i: (0, f)
    outs = pl.pallas_call(
        functools.partial(_ffn_up_kernel, tiles_per_seq=tiles_per_seq, rs_s=rs_s),
        grid=(n_tiles, m // tm),
        in_specs=[
            pl.BlockSpec((tm, D_MODEL), lambda f, i: (i, 0)),
            pl.BlockSpec((ms, D_MODEL), lambda f, i: (0, 0)),
            pl.BlockSpec((None, D_MODEL, tf), lambda f, i: (li, 0, f)),
            pl.BlockSpec((None, D_MODEL, tf), lambda f, i: (li, 0, f)),
            pl.BlockSpec((None, slab, D_MODEL), lambda f, i: (li, f, 0), pipeline_mode=pl.Buffered(1)),
            pl.BlockSpec((None, CONV_W, tf), lambda f, i: (li, 0, f)),
            pl.BlockSpec((None, 1, tf), lambda f, i: (li, 0, f)),
            pl.BlockSpec((cr_s, tf), col),
        ],
        out_specs=[pl.BlockSpec((tm, tf), lambda f, i: (i, f)),
                   pl.BlockSpec((ms, tf), col),
                   pl.BlockSpec((1, cr, tf), lambda f, i: (i // tiles_per_seq, 0, f)),
                   pl.BlockSpec((cr_s, tf), col),
                   pl.BlockSpec((D_MODEL, tf), col),
                   pl.BlockSpec((D_MODEL, tf), col),
                   pl.BlockSpec((slab, D_MODEL), lambda f, i: (f, 0))],
        out_shape=[jax.ShapeDtypeStruct((m, D_FF), BF16), jax.ShapeDtypeStruct((ms, D_FF), BF16),
                   jax.ShapeDtypeStruct((nseq, cr, D_FF), F32), jax.ShapeDtypeStruct((cr_s, D_FF), F32),
                   jax.ShapeDtypeStruct((D_MODEL, D_FF), BF16), jax.ShapeDtypeStruct((D_MODEL, D_FF), BF16),
                   jax.ShapeDtypeStruct((D_FF, D_MODEL), BF16)],
        scratch_shapes=[pltpu.VMEM((cr, tf), F32), pltpu.VMEM((cr_s, tf), F32)],
        compiler_params=_params(("arbitrary", "arbitrary")),
        name="ffn_up",
    )(h, hs, wa, wg, wd, conv_w, conv_b.reshape(DEPTH, 1, D_FF), init_s)
    act, act_s, tail, tail_s, _, _, wd_b = outs
    return act, act_s, tail, tail_s, wd_b


def _s5_prep_kernel(lre_ref, lim_ref, step_ref, lre_x_ref, lim_x_ref, step_x_ref, bre_ref, bim_ref, cre_ref, cim_ref,
                    lbr_ref, lbi_ref, bdr_ref, bdi_ref, cdr_ref, cdi_ref):
    def lam_bar(lre, lim, log_step):
        delta = jnp.exp(log_step)
        mag = jnp.exp(lre * delta)
        return mag * jnp.cos(lim * delta), mag * jnp.sin(lim * delta)

    lbr, lbi = lam_bar(lre_ref[...], lim_ref[...], step_ref[...])
    lbr_ref[...] = lbr
    lbi_ref[...] = lbi
    lre, lim = lre_x_ref[...], lim_x_ref[...]
    xr, xi = lam_bar(lre, lim, step_x_ref[...])
    nr, ni = xr - 1.0, xi
    den = lre * lre + lim * lim
    cr = (nr * lre + ni * lim) / den
    ci = (ni * lre - nr * lim) / den
    br, bi = bre_ref[...], bim_ref[...]

    def spread(w, group_rows, group_cols):
        rows, width = w.shape
        cols = (rows // group_rows) * group_cols
        idx = lambda shp, d: lax.broadcasted_iota(jnp.int32, shp, d)
        repeat = (idx((width, cols), 1) % width == idx((width, cols), 0)).astype(BF16)
        own = idx((rows, cols), 0) // group_rows == idx((rows, cols), 1) // group_cols
        tiled = jnp.dot(w.astype(BF16), repeat, preferred_element_type=F32)
        return jnp.where(own, tiled, 0.0).astype(BF16)

    bdr_ref[0] = spread(cr * br - ci * bi, SSM_GROUP_CH, SSM_P)
    bdi_ref[0] = spread(cr * bi + ci * br, SSM_GROUP_CH, SSM_P)
    cdr_ref[0] = spread(cre_ref[...], SSM_P, SSM_GROUP_CH)
    cdi_ref[0] = spread(cim_ref[...], SSM_P, SSM_GROUP_CH)


def s5_prep(lam_re, lam_im, log_step, b_re, b_im, c_re, c_im):
    gpb = SSM_GROUPS // SSM_BLOCKS
    step = jnp.broadcast_to(log_step[:, None], (SSM_GROUPS, SSM_P))
    rep = lambda a: jnp.repeat(a, SSM_GROUP_CH, axis=0)
    by_ch = lambda b: b.transpose(0, 2, 1).reshape(SSM_GROUPS * SSM_GROUP_CH, SSM_P)
    by_p = lambda c: c.transpose(0, 2, 1).reshape(SSM_GROUPS * SSM_P, SSM_GROUP_CH)
    blk = lambda rows, width: pl.BlockSpec((rows, width), lambda k: (k, 0))
    mat = lambda rows, cols: pl.BlockSpec((1, rows, cols), lambda k: (k, 0, 0))
    return pl.pallas_call(
        _s5_prep_kernel,
        grid=(SSM_BLOCKS,),
        in_specs=[blk(gpb, SSM_P)] * 3 + [blk(SSM_CH_BLOCK, SSM_P)] * 5 + [blk(SSM_LANE_BLOCK, SSM_GROUP_CH)] * 2,
        out_specs=[blk(gpb, SSM_P), blk(gpb, SSM_P),
                   mat(SSM_CH_BLOCK, SSM_LANE_BLOCK), mat(SSM_CH_BLOCK, SSM_LANE_BLOCK),
                   mat(SSM_LANE_BLOCK, SSM_CH_BLOCK), mat(SSM_LANE_BLOCK, SSM_CH_BLOCK)],
        out_shape=[jax.ShapeDtypeStruct((SSM_GROUPS, SSM_P), F32)] * 2
        + [jax.ShapeDtypeStruct((SSM_BLOCKS, SSM_CH_BLOCK, SSM_LANE_BLOCK), BF16)] * 2
        + [jax.ShapeDtypeStruct((SSM_BLOCKS, SSM_LANE_BLOCK, SSM_CH_BLOCK), BF16)] * 2,
        compiler_params=_params(("parallel",)),
        name="s5_prep",
    )(lam_re, lam_im, step, rep(lam_re), rep(lam_im), rep(step), by_ch(b_re), by_ch(b_im), by_p(c_re), by_p(c_im))


def _s5_dense_kernel(x_ref, gpre_ref, d_ref, lbr_ref, lbi_ref, bdr_ref, bdi_ref, cdr_ref, cdi_ref, s0r_ref, s0i_ref,
                     z_ref, sr_ref, si_ref, ur_ref, ui_ref, *, nb, tt):
    h = _rms(x_ref[...], gpre_ref[...])
    hb = h.astype(BF16)
    for kb in range(SSM_BLOCKS):
        ch = slice(kb * SSM_CH_BLOCK, (kb + 1) * SSM_CH_BLOCK)
        ln = slice(kb * SSM_LANE_BLOCK, (kb + 1) * SSM_LANE_BLOCK)
        ur_ref[...] = jnp.dot(hb[:, ch], bdr_ref[kb], preferred_element_type=F32)
        ui_ref[...] = jnp.dot(hb[:, ch], bdi_ref[kb], preferred_element_type=F32)
        a_re = jnp.broadcast_to(lbr_ref[:, ln], (nb, SSM_LANE_BLOCK))
        a_im = jnp.broadcast_to(lbi_ref[:, ln], (nb, SSM_LANE_BLOCK))
        s_re, s_im = s0r_ref[:, ln], s0i_ref[:, ln]
        for t in range(tt):
            rows = slice(t * nb, (t + 1) * nb)
            s_re, s_im = ((a_re * s_re - a_im * s_im) + ur_ref[rows, :],
                          (a_re * s_im + a_im * s_re) + ui_ref[rows, :])
            ur_ref[rows, :] = s_re
            ui_ref[rows, :] = s_im
        sr_ref[:, ln] = s_re
        si_ref[:, ln] = s_im
        y = (jnp.dot(ur_ref[...].astype(BF16), cdr_ref[kb], preferred_element_type=F32)
             - jnp.dot(ui_ref[...].astype(BF16), cdi_ref[kb], preferred_element_type=F32))
        y = y + d_ref[:, ch] * h[:, ch]
        z_ref[:, ch] = jax.nn.gelu(y).astype(z_ref.dtype)


def s5_dense(x, g_pre, d_skip, lbr, lbi, bdr, bdi, cdr, cdi, s0r, s0i, nb, tt):
    rows = nb * tt
    full = lambda shp: pl.BlockSpec(shp, lambda: tuple(0 for _ in shp))
    bd = full((SSM_BLOCKS, SSM_CH_BLOCK, SSM_LANE_BLOCK))
    cd = full((SSM_BLOCKS, SSM_LANE_BLOCK, SSM_CH_BLOCK))
    return pl.pallas_call(
        functools.partial(_s5_dense_kernel, nb=nb, tt=tt),
        in_specs=[full((rows, D_MODEL)), full((1, D_MODEL)), full((1, D_MODEL)), full((1, SSM_STATE)),
                  full((1, SSM_STATE)), bd, bd, cd, cd, full((nb, SSM_STATE)), full((nb, SSM_STATE))],
        out_specs=[full((rows, D_MODEL)), full((nb, SSM_STATE)), full((nb, SSM_STATE))],
        out_shape=[jax.ShapeDtypeStruct((rows, D_MODEL), BF16),
                   jax.ShapeDtypeStruct((nb, SSM_STATE), F32), jax.ShapeDtypeStruct((nb, SSM_STATE), F32)],
        scratch_shapes=[pltpu.VMEM((rows, SSM_LANE_BLOCK), F32), pltpu.VMEM((rows, SSM_LANE_BLOCK), F32)],
        compiler_params=pltpu.CompilerParams(vmem_limit_bytes=VMEM_LIMIT),
        name="s5_dense",
    )(x, g_pre.reshape(1, D_MODEL), d_skip.reshape(1, D_MODEL), lbr.reshape(1, SSM_STATE), lbi.reshape(1, SSM_STATE),
      bdr, bdi, cdr, cdi, s0r, s0i)


def _s5_seq_kernel(x_ref, gpre_ref, d_ref, lam_r_ref, lam_i_ref, bdr_ref, bdi_ref, cdr_ref, cdi_ref,
                   z_ref, sr_ref, si_ref, ur_ref, ui_ref, hs_ref, yo_ref):
    tt, pitch = S5_TT, S5_PITCH
    ti = pl.program_id(1)

    @pl.when(ti == 0)
    def _():
        sr_ref[...] = jnp.zeros(sr_ref.shape, F32)
        si_ref[...] = jnp.zeros(si_ref.shape, F32)

    h = _rms(x_ref[...], gpre_ref[...])
    zeros8 = jnp.zeros((SUBLANES, D_MODEL), F32)
    hs_ref[0:SUBLANES, :] = zeros8
    hs_ref[SUBLANES + tt:2 * SUBLANES + tt, :] = zeros8
    hs_ref[SUBLANES:SUBLANES + tt, :] = h
    hb = h.astype(BF16)
    hb_shift = hs_ref[4:tt + 12, :].astype(BF16)

    def even_rows(kb, lt):
        return pl.ds((kb * SUBLANES + 2 * lt) * pitch, tt)

    def odd_rows(kb, lt):
        return pl.ds((kb * SUBLANES + 2 * lt + 1) * pitch - 4, tt + SUBLANES)

    for kb in range(SSM_BLOCKS):
        ch = slice(kb * SSM_CH_BLOCK, (kb + 1) * SSM_CH_BLOCK)
        for u_ref, bd_ref in ((ur_ref, bdr_ref), (ui_ref, bdi_ref)):
            ue = jnp.dot(hb[:, ch], bd_ref[kb, :, 0:SSM_HALF], preferred_element_type=F32)
            uo = jnp.dot(hb_shift[:, ch], bd_ref[kb, :, SSM_HALF:SSM_LANE_BLOCK], preferred_element_type=F32)
            for lt in range(4):
                u_ref[even_rows(kb, lt), :] = ue[:, lt * LANES:(lt + 1) * LANES]
                u_ref[odd_rows(kb, lt), :] = uo[:, lt * LANES:(lt + 1) * LANES]

    blk = lambda q: slice(q * SUBLANES, (q + 1) * SUBLANES)
    a_re = [lam_r_ref[blk(q), :] for q in range(SSM_BLOCKS)]
    a_im = [lam_i_ref[blk(q), :] for q in range(SSM_BLOCKS)]

    def step(t, carry):
        new = []
        for q in range(SSM_BLOCKS):
            s_re, s_im = carry[2 * q], carry[2 * q + 1]
            rows = pl.ds(q * SUBLANES * pitch + t, SUBLANES, stride=pitch)
            n_re = (a_re[q] * s_re - a_im[q] * s_im) + ur_ref[rows, :]
            n_im = (a_re[q] * s_im + a_im[q] * s_re) + ui_ref[rows, :]
            ur_ref[rows, :] = n_re
            ui_ref[rows, :] = n_im
            new += [n_re, n_im]
        return tuple(new)

    init = []
    for q in range(SSM_BLOCKS):
        init += [sr_ref[0, blk(q), :], si_ref[0, blk(q), :]]
    final = lax.fori_loop(0, tt, step, tuple(init), unroll=2)
    for q in range(SSM_BLOCKS):
        sr_ref[0, blk(q), :] = final[2 * q]
        si_ref[0, blk(q), :] = final[2 * q + 1]

    for kb in range(SSM_BLOCKS):
        ch = slice(kb * SSM_CH_BLOCK, (kb + 1) * SSM_CH_BLOCK)

        def gather(u_ref, rows_of):
            return jnp.concatenate([u_ref[rows_of(kb, lt), :] for lt in range(4)], axis=1).astype(BF16)

        y_even = (jnp.dot(gather(ur_ref, even_rows), cdr_ref[kb, 0:SSM_HALF, :], preferred_element_type=F32)
                  - jnp.dot(gather(ui_ref, even_rows), cdi_ref[kb, 0:SSM_HALF, :], preferred_element_type=F32))
        yo_ref[...] = (jnp.dot(gather(ur_ref, odd_rows), cdr_ref[kb, SSM_HALF:SSM_LANE_BLOCK, :],
                               preferred_element_type=F32)
                       - jnp.dot(gather(ui_ref, odd_rows), cdi_ref[kb, SSM_HALF:SSM_LANE_BLOCK, :],
                                 preferred_element_type=F32))
        y = y_even + yo_ref[4:tt + 4, :]
        y = y + d_ref[:, ch] * h[:, ch]
        z_ref[:, ch] = jax.nn.gelu(y).astype(z_ref.dtype)


def s5_seq(x, g_pre, d_skip, lam_r, lam_i, bdr, bdi, cdr, cdi, nseq):
    m = x.shape[0]
    ntt = m // (nseq * S5_TT)
    fixed2 = lambda shp: pl.BlockSpec(shp, lambda si, ti: (0, 0))
    state = pl.BlockSpec((1, SSM_SLABS, LANES), lambda si, ti: (si, 0, 0))
    bd = _resident((SSM_BLOCKS, SSM_CH_BLOCK, SSM_LANE_BLOCK), lambda si, ti: (0, 0, 0))
    cd = _resident((SSM_BLOCKS, SSM_LANE_BLOCK, SSM_CH_BLOCK), lambda si, ti: (0, 0, 0))
    return pl.pallas_call(
        _s5_seq_kernel,
        grid=(nseq, ntt),
        in_specs=[pl.BlockSpec((S5_TT, D_MODEL), lambda si, ti: (si * ntt + ti, 0)),
                  fixed2((1, D_MODEL)), fixed2((1, D_MODEL)), fixed2((SSM_SLABS, LANES)), fixed2((SSM_SLABS, LANES)),
                  bd, bd, cd, cd],
        out_specs=[pl.BlockSpec((S5_TT, D_MODEL), lambda si, ti: (si * ntt + ti, 0)), state, state],
        out_shape=[jax.ShapeDtypeStruct((m, D_MODEL), BF16),
                   jax.ShapeDtypeStruct((nseq, SSM_SLABS, LANES), F32),
                   jax.ShapeDtypeStruct((nseq, SSM_SLABS, LANES), F32)],
        scratch_shapes=[pltpu.VMEM((SSM_SLABS * S5_PITCH, LANES), F32), pltpu.VMEM((SSM_SLABS * S5_PITCH, LANES), F32),
                        pltpu.VMEM((S5_TT + 2 * SUBLANES, D_MODEL), F32),
                        pltpu.VMEM((S5_TT + SUBLANES, SSM_CH_BLOCK), F32)],
        compiler_params=_params(("parallel", "arbitrary")),
        name="s5_seq",
    )(x, g_pre.reshape(1, D_MODEL), d_skip.reshape(1, D_MODEL), lam_r, lam_i, bdr, bdi, cdr, cdi)


def _to_slabs(v):
    lead = v.shape[:-1]
    v = v.reshape(lead + (SSM_BLOCKS, SUBLANES, LANES))
    return jnp.take(v, jnp.array(SLAB_LANE_TILE), axis=-2).reshape(lead + (SSM_SLABS, LANES))


def _from_slabs(s):
    lead = s.shape[:-2]
    s = s.reshape(lead + (SSM_BLOCKS, SUBLANES, LANES))
    return jnp.take(s, jnp.array(LANE_TILE_SLAB), axis=-2).reshape(lead + (SSM_STATE,))


def _slopes():
    return 2.0 ** (-8.0 * jnp.arange(1, SWA_Q_HEADS + 1, dtype=F32) / SWA_Q_HEADS)


def kernel(x_prompt, x_sample, state_ret, cache_swa_k, cache_swa_v, state_ssm_re, state_ssm_im, state_ffn_conv, norm_mix_pre, norm_mix_post, norm_ffn_pre, norm_ffn_post, w_in_even, w_out_even, swa_sinks, ssm_lam_re, ssm_lam_im, ssm_log_step, ssm_b_re, ssm_b_im, ssm_c_re, ssm_c_im, ssm_d, w_glu, ffn_w_a, ffn_w_g, ffn_conv_w, ffn_conv_b, ffn_w_down):
    pb, pl_len, _ = x_prompt.shape
    sb, sl_len, _ = x_sample.shape
    mp = pb * pl_len
    ms = sb * sl_len
    s_pad = BF16_ROWS
    carry_rows_s = (CONV_W - 1) * sb
    slopes = _slopes()

    xp = x_prompt.reshape(mp, D_MODEL)
    xs = x_sample.transpose(1, 0, 2).reshape(ms, D_MODEL)

    hp = rmsnorm_bf16(xp, norm_mix_pre[0], 512)
    hs = rmsnorm_bf16(xs, norm_mix_pre[0], ms)

    ret_p, ret_s, wk_p, wk_s, wv_p, wv_s = [], [], [], [], [], []
    sre_p, sre_s, sim_p, sim_s, conv_p, conv_s = [], [], [], [], [], []
    w_glu_b = None

    for layer in range(DEPTH):
        i = layer // 2
        if layer % 2 == 0:
            proj, proj_s, w_out_b = in_proj(hp, hs, w_in_even, w_out_even, i, 1024, 768)
            proj = proj.reshape(pb, pl_len, EVEN_IN)
            o_ret, s_ret = retention(proj, jnp.zeros((pb, RET_HEADS, RET_DK, RET_DV), F32), RET_CHUNK, RET_CHUNK,
                                     1, RET_CHUNKS_PER_STEP)
            kcol, vcol = KV_COL // LANES, KV_COL // LANES + 1
            blk = lambda c, prev: pl.BlockSpec(
                (1, SWA_BLOCK, LANES),
                (lambda bi, qi: (bi, jnp.maximum(qi - 1, 0), c)) if prev else (lambda bi, qi: (bi, qi, c)))
            o_swa = swa_attention(
                slopes, swa_sinks[i], proj,
                pl.BlockSpec((1, SWA_BLOCK, SWA_WIDTH), lambda bi, qi: (bi, qi, Q_COL // SWA_WIDTH)),
                [proj, proj, proj, proj], [blk(kcol, True), blk(vcol, True), blk(kcol, False), blk(vcol, False)],
                pb, 1, 1, pl_len // SWA_BLOCK, SWA_BLOCK, 0, SWA_BLOCK)
            ret_p.append(s_ret)
            kv_shape = (pb, WINDOW, SWA_KV_HEADS, SWA_HD)
            wk_p.append(proj[:, pl_len - WINDOW:, KV_COL:KV_COL + KV_WIDTH].reshape(kv_shape))
            wv_p.append(proj[:, pl_len - WINDOW:, KV_COL + KV_WIDTH:KV_COL + 2 * KV_WIDTH].reshape(kv_shape))

            proj_bt = proj_s.reshape(sl_len, sb, EVEN_IN).transpose(1, 0, 2)
            proj_pad = jnp.pad(proj_bt, ((0, 0), (0, s_pad - sl_len), (0, 0)))
            o_ret_s, s_ret_s = retention(proj_pad, state_ret[i], sl_len, s_pad, SAMPLE_SEQS_PER_STEP, 1)
            win = cache_swa_k.shape[2]
            ck = cache_swa_k[i].reshape(sb, win, KV_WIDTH)
            cv = cache_swa_v[i].reshape(sb, win, KV_WIDTH)
            cache_spec = pl.BlockSpec((SAMPLE_SEQS_PER_STEP, win, KV_WIDTH), lambda bi, qi: (bi, 0, 0))
            cur = lambda c: pl.BlockSpec((SAMPLE_SEQS_PER_STEP, s_pad, LANES), lambda bi, qi: (bi, 0, c))
            o_swa_s = swa_attention(
                slopes, swa_sinks[i], proj_pad,
                pl.BlockSpec((SAMPLE_SEQS_PER_STEP, s_pad, SWA_WIDTH), lambda bi, qi: (bi, 0, Q_COL // SWA_WIDTH)),
                [ck, cv, proj_pad, proj_pad], [cache_spec, cache_spec, cur(kcol), cur(vcol)],
                sb, SAMPLE_SEQS_PER_STEP, SWA_CHAINS, 1, s_pad, win, 0)
            ret_s.append(s_ret_s)
            k_new = proj_bt[:, :, KV_COL:KV_COL + KV_WIDTH]
            v_new = proj_bt[:, :, KV_COL + KV_WIDTH:KV_COL + 2 * KV_WIDTH]
            kv_shape_s = (sb, win, SWA_KV_HEADS, SWA_HD)
            wk_s.append(jnp.concatenate([ck, k_new], axis=1)[:, sl_len:].reshape(kv_shape_s))
            wv_s.append(jnp.concatenate([cv, v_new], axis=1)[:, sl_len:].reshape(kv_shape_s))
            tb = lambda o: o[:, :sl_len].transpose(1, 0, 2).reshape(ms, -1)

            xp, xs, hp, hs, _ = proj_residual(
                [o_ret.reshape(mp, RET_WIDTH), o_swa.reshape(mp, SWA_WIDTH)], [tb(o_ret_s), tb(o_swa_s)],
                w_out_b, [0, 1], False, xp, xs, norm_mix_post[layer], norm_ffn_pre[layer], 512, "out_proj")
        else:
            lbr, lbi, bdr, bdi, cdr, cdi = s5_prep(ssm_lam_re[i], ssm_lam_im[i], ssm_log_step[i], ssm_b_re[i],
                                                   ssm_b_im[i], ssm_c_re[i], ssm_c_im[i])
            lam_r = _to_slabs(lbr.reshape(SSM_STATE))
            lam_i = _to_slabs(lbi.reshape(SSM_STATE))
            zp, s_re, s_im = s5_seq(xp, norm_mix_pre[layer], ssm_d[i], lam_r, lam_i, bdr, bdi, cdr, cdi, pb)
            sre_p.append(_from_slabs(s_re).reshape(pb, SSM_GROUPS, SSM_P))
            sim_p.append(_from_slabs(s_im).reshape(pb, SSM_GROUPS, SSM_P))
            zs, s_re_s, s_im_s = s5_dense(xs, norm_mix_pre[layer], ssm_d[i], lbr, lbi, bdr, bdi, cdr, cdi,
                                          state_ssm_re[i].reshape(sb, SSM_STATE),
                                          state_ssm_im[i].reshape(sb, SSM_STATE), sb, sl_len)
            sre_s.append(s_re_s.reshape(sb, SSM_GROUPS, SSM_P))
            sim_s.append(s_im_s.reshape(sb, SSM_GROUPS, SSM_P))
            xp, xs, hp, hs, _ = proj_residual([zp], [zs], w_glu_b, [0], True, xp, xs, norm_mix_post[layer],
                                              norm_ffn_pre[layer], 256, "glu_proj")

        init_s = state_ffn_conv[layer].transpose(1, 0, 2).reshape(carry_rows_s, D_FF)
        act, act_s, tail, tail_s, w_down_b = ffn_up(hp, hs, ffn_w_a, ffn_w_g, ffn_w_down, layer, ffn_conv_w,
                                                    ffn_conv_b, init_s, sb, 1024, 512, pl_len)
        conv_p.append(tail[:, SUBLANES - (CONV_W - 1):])
        conv_s.append(tail_s.reshape(CONV_W - 1, sb, D_FF).transpose(1, 0, 2))
        next_even = layer + 1 < DEPTH and (layer + 1) % 2 == 0
        next_odd = layer + 1 < DEPTH and (layer + 1) % 2 == 1
        xp, xs, hp, hs, w_cast = proj_residual(
            [act], [act_s], w_down_b, [0], False, xp, xs, norm_ffn_post[layer],
            norm_mix_pre[layer + 1] if next_even else None, 256, "ffn_down",
            cast=(w_glu, (layer + 1) // 2) if next_odd else None)
        if next_odd:
            w_glu_b = w_cast

    y_prompt = xp.reshape(pb, pl_len, D_MODEL)
    y_sample = xs.reshape(sl_len, sb, D_MODEL).transpose(1, 0, 2)
    return (y_prompt, y_sample, jnp.stack(ret_p), jnp.stack(ret_s), jnp.stack(wk_p), jnp.stack(wk_s),
            jnp.stack(wv_p), jnp.stack(wv_s), jnp.stack(sre_p), jnp.stack(sre_s), jnp.stack(sim_p),
            jnp.stack(sim_s), jnp.stack(conv_p), jnp.stack(conv_s))
```

```python
import functools

import jax
import jax.numpy as jnp
from jax import lax
from jax.experimental import pallas as pl
from jax.experimental.pallas import tpu as pltpu

F32 = jnp.float32
BF16 = jnp.bfloat16

D_MODEL = 2048
DEPTH = 4
RET_HEADS = 8
RET_DK = 128
RET_DV = 128
RET_CHUNK = 128
RET_WIDTH = RET_HEADS * RET_DV
SWA_Q_HEADS = 16
SWA_KV_HEADS = 2
SWA_HD = 64
WINDOW = 128
SWA_BLOCK = 128
SWA_WIDTH = SWA_Q_HEADS * SWA_HD
SWA_CHAINS = 4
RET_CHUNKS_PER_STEP = 4
SAMPLE_SEQS_PER_STEP = 4
EVEN_IN = 5376
Q_COL = RET_WIDTH * 4
KV_COL = Q_COL + SWA_WIDTH
KV_WIDTH = SWA_KV_HEADS * SWA_HD
SSM_GROUP_CH = 16
SSM_GROUPS = 128
SSM_P = 64
SSM_STATE = SSM_GROUPS * SSM_P
SSM_LANE_BLOCK = 1024
SSM_CH_BLOCK = 256
SSM_BLOCKS = SSM_STATE // SSM_LANE_BLOCK
SSM_HALF = SSM_LANE_BLOCK // 2
D_FF = 5632
CONV_W = 3
NORM_EPS = 1e-6

LANES = 128
SUBLANES = 8
BF16_ROWS = 16
VMEM_LIMIT = 56 * 1024 * 1024

SSM_SLABS = SSM_STATE // LANES
S5_TT = 256
S5_PITCH = S5_TT + 4
SLAB_LANE_TILE = (0, 4, 1, 5, 2, 6, 3, 7)
LANE_TILE_SLAB = (0, 2, 4, 6, 1, 3, 5, 7)


def _params(sem, vmem=VMEM_LIMIT):
    return pltpu.CompilerParams(dimension_semantics=sem, vmem_limit_bytes=vmem)


def _rms(x, g):
    return x * lax.rsqrt(jnp.mean(x * x, axis=-1, keepdims=True) + NORM_EPS) * g


def _resident(shape, index_map):
    return pl.BlockSpec(shape, index_map, pipeline_mode=pl.Buffered(1))


def _norm_kernel(x_ref, g_ref, o_ref):
    o_ref[...] = _rms(x_ref[...], g_ref[...]).astype(o_ref.dtype)


def rmsnorm_bf16(x, g, tm):
    m, d = x.shape
    return pl.pallas_call(
        _norm_kernel,
        grid=(m // tm,),
        in_specs=[pl.BlockSpec((tm, d), lambda i: (i, 0)), pl.BlockSpec((1, d), lambda i: (0, 0))],
        out_specs=pl.BlockSpec((tm, d), lambda i: (i, 0)),
        out_shape=jax.ShapeDtypeStruct((m, d), BF16),
        compiler_params=_params(("parallel",)),
        name="rmsnorm",
    )(x, g.reshape(1, d))


def _in_proj_kernel(x_ref, xs_ref, w_ref, wo_ref, o_ref, os_ref, wob_ref, wb_ref):
    @pl.when(pl.program_id(1) == 0)
    def _():
        wb_ref[...] = w_ref[...].astype(BF16)
        wob_ref[...] = wo_ref[...].astype(BF16)
        os_ref[...] = jnp.dot(xs_ref[...], wb_ref[...], preferred_element_type=F32)

    o_ref[...] = jnp.dot(x_ref[...], wb_ref[...], preferred_element_type=F32)


def in_proj(h, hs, w_in, w_out, li, tm, tn):
    m, ms = h.shape[0], hs.shape[0]
    n_tiles = EVEN_IN // tn
    slab = 512
    n_slabs = D_MODEL // slab
    assert n_slabs <= n_tiles
    slab_idx = lambda j, i: (li, jnp.minimum(j, n_slabs - 1), 0)
    return pl.pallas_call(
        _in_proj_kernel,
        grid=(n_tiles, m // tm),
        in_specs=[pl.BlockSpec((tm, D_MODEL), lambda j, i: (i, 0)),
                  pl.BlockSpec((ms, D_MODEL), lambda j, i: (0, 0)),
                  pl.BlockSpec((None, D_MODEL, tn), lambda j, i: (li, 0, j)),
                  pl.BlockSpec((None, slab, D_MODEL), slab_idx)],
        out_specs=[pl.BlockSpec((tm, tn), lambda j, i: (i, j)),
                   pl.BlockSpec((ms, tn), lambda j, i: (0, j)),
                   pl.BlockSpec((slab, D_MODEL), lambda j, i: (jnp.minimum(j, n_slabs - 1), 0))],
        out_shape=[jax.ShapeDtypeStruct((m, EVEN_IN), F32), jax.ShapeDtypeStruct((ms, EVEN_IN), F32),
                   jax.ShapeDtypeStruct((D_MODEL, D_MODEL), BF16)],
        scratch_shapes=[pltpu.VMEM((D_MODEL, tn), BF16)],
        compiler_params=_params(("arbitrary", "arbitrary")),
        name="in_proj",
    )(h, hs, w_in, w_out)


def _retention_kernel(decay_ref, q_ref, k_ref, v_ref, g_ref, intra_ref, read_ref, write_ref, s0_ref,
                      o_ref, s_ref, *, nseq, chunks, c):
    ci = pl.program_id(1)

    @pl.when(ci == 0)
    def _():
        s_ref[...] = s0_ref[...]

    heads = [slice(h * RET_DK, (h + 1) * RET_DK) for h in range(RET_HEADS)]
    nt = (((1,), (1,)), ((), ()))
    tn = (((0,), (0,)), ((), ()))
    jobs = [(b, h) for b in range(nseq) for h in range(RET_HEADS)]
    for cc in range(chunks):
        rows = slice(cc * c, (cc + 1) * c)
        first = []
        for b, h in jobs:
            sl = heads[h]
            q = q_ref[b, rows, sl].astype(BF16)
            k = k_ref[b, rows, sl] * (RET_DK ** -0.5)
            v = v_ref[b, rows, sl].astype(BF16)
            s = s_ref[b, h]
            sc = lax.dot_general(q, k.astype(BF16), nt, preferred_element_type=F32)
            qs = jnp.dot(q, s.astype(BF16), preferred_element_type=F32)
            kv = lax.dot_general((k * write_ref[h]).astype(BF16), v, tn, preferred_element_type=F32)
            first.append((v, s, sc, qs, kv))
        outs = []
        for (b, h), (v, s, sc, qs, kv) in zip(jobs, first):
            outs.append(jnp.dot((sc * intra_ref[h]).astype(BF16), v, preferred_element_type=F32) + qs * read_ref[h])
            s_ref[b, h] = s * decay_ref[h] + kv
        for (b, h), o in zip(jobs, outs):
            g = g_ref[b, rows, heads[h]]
            o = o * lax.rsqrt(jnp.mean(o * o, axis=-1, keepdims=True) + NORM_EPS)
            o_ref[b, rows, heads[h]] = (o * (g * jax.nn.sigmoid(g))).astype(o_ref.dtype)


def _retention_tables(c_real, c_pad):
    lg = jnp.log(1.0 - 2.0 ** (-5.0 - jnp.arange(RET_HEADS, dtype=F32)))
    idx = jnp.arange(c_pad, dtype=F32)
    diff = idx[:, None] - idx[None, :]
    intra = jnp.where(diff >= 0, jnp.exp(lg[:, None, None] * jnp.maximum(diff, 0.0)), 0.0)
    read = jnp.exp(lg[:, None] * (idx[None, :] + 1.0))
    write = jnp.exp(lg[:, None] * (c_real - 1.0 - idx[None, :]))
    decay = jnp.exp(lg * c_real)
    bshape = (RET_HEADS, c_pad, RET_DV)
    return decay, intra, jnp.broadcast_to(read[:, :, None], bshape), jnp.broadcast_to(write[:, :, None], bshape)


def retention(proj, s0, li, c_real, c_pad, nseq, chunks):
    b, l, _ = proj.shape
    rows = chunks * c_pad
    decay, intra, read, write = _retention_tables(c_real, c_pad)

    def col(j):
        return pl.BlockSpec((nseq, rows, RET_WIDTH), lambda bi, ci: (bi, ci, j))

    state_shape = (nseq, RET_HEADS, RET_DK, RET_DV)
    return pl.pallas_call(
        functools.partial(_retention_kernel, nseq=nseq, chunks=chunks, c=c_pad),
        grid=(b // nseq, l // rows),
        in_specs=[
            pl.BlockSpec(memory_space=pltpu.SMEM),
            col(0), col(1), col(2), col(3),
            pl.BlockSpec((RET_HEADS, c_pad, c_pad), lambda bi, ci: (0, 0, 0)),
            pl.BlockSpec((RET_HEADS, c_pad, RET_DV), lambda bi, ci: (0, 0, 0)),
            pl.BlockSpec((RET_HEADS, c_pad, RET_DV), lambda bi, ci: (0, 0, 0)),
            pl.BlockSpec((None,) + state_shape, lambda bi, ci: (li, bi, 0, 0, 0)),
        ],
        out_specs=[pl.BlockSpec((nseq, rows, RET_WIDTH), lambda bi, ci: (bi, ci, 0)),
                   pl.BlockSpec(state_shape, lambda bi, ci: (bi, 0, 0, 0))],
        out_shape=[jax.ShapeDtypeStruct((b, l, RET_WIDTH), BF16), jax.ShapeDtypeStruct(s0.shape[1:], F32)],
        compiler_params=_params(("parallel", "arbitrary")),
        name="retention",
    )(decay, proj, proj, proj, proj, intra, read, write, s0)


def _swa_kernel(bias_p_ref, bias_c_ref, sink_ref, q_ref, kp_ref, vp_ref, kc_ref, vc_ref, o_ref, *,
                nseq, group, tq, q0_base, q0_step):
    prev_always_visible = q0_step == 0 and q0_base >= WINDOW
    if not prev_always_visible:
        q0 = q0_base + pl.program_id(1) * q0_step
        prev_penalty = jnp.where(q0 >= WINDOW, 0.0, -jnp.inf).astype(F32)
    lane_lo = lax.broadcasted_iota(jnp.int32, (tq, LANES), 1) < SWA_HD
    scale = SWA_HD ** -0.5
    nt = (((1,), (1,)), ((), ()))
    pairs_per_kv = SWA_Q_HEADS // SWA_KV_HEADS // 2
    lane_sum = lambda x: jnp.sum(x, axis=-1, keepdims=True)
    lane_max = lambda x: jnp.max(x, axis=-1, keepdims=True)

    def both(ref, b):
        x = ref[b]
        return x.astype(BF16), pltpu.roll(x, SWA_HD, 1).astype(BF16)

    keys = [(both(kp_ref, b), both(kc_ref, b)) for b in range(nseq)]
    vals = [(both(vp_ref, b), both(vc_ref, b)) for b in range(nseq)]
    all_chains = [(b, kv, parity) for b in range(nseq) for kv in range(SWA_KV_HEADS) for parity in range(2)]
    outs = {}
    for g0 in range(0, len(all_chains), group):
        chains = all_chains[g0:g0 + group]
        scores = []
        for b, kv, parity in chains:
            swapped = (kv == 0) != (parity == 0)
            (kp, kp_sw), (kc, kc_sw) = keys[b]
            kprev, kcur = (kp_sw, kc_sw) if swapped else (kp, kc)
            qs = []
            for p in range(pairs_per_kv):
                pair = kv * pairs_per_kv + p
                qp = q_ref[b, :, pair * LANES:(pair + 1) * LANES]
                qs.append(jnp.where(lane_lo, qp, 0.0) if parity == 0 else jnp.where(lane_lo, 0.0, qp))
            qh = jnp.concatenate(qs, axis=0).astype(BF16)
            scores.append((lax.dot_general(qh, kprev, nt, preferred_element_type=F32),
                           lax.dot_general(qh, kcur, nt, preferred_element_type=F32)))
        probs = []
        for (b, kv, parity), (sp, sc) in zip(chains, scores):
            chain = 2 * kv + parity
            sink = sink_ref[chain][:, :1]
            sp = sp * scale + bias_p_ref[chain]
            sc = sc * scale + bias_c_ref[chain]
            if not prev_always_visible:
                sp = sp + prev_penalty
            if tq == WINDOW:
                m = jnp.maximum(lane_max(jnp.maximum(sp, sc)), sink)
            else:
                m = jnp.maximum(jnp.maximum(lane_max(sp), lane_max(sc)), sink)
            pp = jnp.exp(sp - m)
            pc = jnp.exp(sc - m)
            psum = lane_sum(pp + pc) if tq == WINDOW else lane_sum(pp) + lane_sum(pc)
            denom = psum + jnp.exp(sink - m)
            probs.append(((pp / denom).astype(BF16), (pc / denom).astype(BF16)))
        for (b, kv, parity), (pp, pc) in zip(chains, probs):
            swapped = (kv == 0) != (parity == 0)
            (vp, vp_sw), (vc, vc_sw) = vals[b]
            vprev, vcur = (vp_sw, vc_sw) if swapped else (vp, vc)
            outs[b, kv, parity] = (jnp.dot(pp, vprev, preferred_element_type=F32)
                                   + jnp.dot(pc, vcur, preferred_element_type=F32))
        for b, kv, parity in chains:
            if parity == 0:
                continue
            for p in range(pairs_per_kv):
                pair = kv * pairs_per_kv + p
                rs = slice(p * tq, (p + 1) * tq)
                o_ref[b, :, pair * LANES:(pair + 1) * LANES] = jnp.where(lane_lo, outs[b, kv, 0][rs],
                                                                          outs[b, kv, 1][rs]).astype(o_ref.dtype)


def _swa_tables(slopes, sinks, tq):
    per = SWA_Q_HEADS // SWA_CHAINS

    def rows_of(vals):
        v = vals.astype(F32).reshape(SWA_KV_HEADS, per, 2).transpose(0, 2, 1).reshape(SWA_CHAINS, per)
        return jnp.broadcast_to(v[:, :, None, None], (SWA_CHAINS, per, tq, 1))

    r = jnp.arange(tq, dtype=jnp.int32)[:, None]
    dist_p = r - jnp.arange(WINDOW, dtype=jnp.int32)[None, :] + WINDOW
    dist_c = r - jnp.arange(tq, dtype=jnp.int32)[None, :]
    slope = rows_of(slopes)
    bias_p = jnp.where(dist_p <= WINDOW, -(slope * dist_p.astype(F32)), -jnp.inf)
    bias_c = jnp.where(dist_c >= 0, -(slope * dist_c.astype(F32)), -jnp.inf)
    sink = jnp.broadcast_to(rows_of(sinks), (SWA_CHAINS, per, tq, LANES))
    flat = lambda t: t.reshape(SWA_CHAINS, per * tq, t.shape[-1])
    return flat(bias_p), flat(bias_c), flat(sink)


def swa_attention(slopes, sinks, q_arr, q_spec, kv_arrs, kv_specs, b, nseq, group, nb, tq, q0_base, q0_step):
    assert q0_base % WINDOW == 0 and q0_step % WINDOW == 0
    rows = (SWA_Q_HEADS // SWA_CHAINS) * tq
    table = lambda width: pl.BlockSpec((SWA_CHAINS, rows, width), lambda bi, i: (0, 0, 0))
    return pl.pallas_call(
        functools.partial(_swa_kernel, nseq=nseq, group=group, tq=tq, q0_base=q0_base, q0_step=q0_step),
        grid=(b // nseq, nb),
        in_specs=[table(WINDOW), table(tq), table(LANES), q_spec] + kv_specs,
        out_specs=pl.BlockSpec((nseq, tq, SWA_WIDTH), lambda bi, i: (bi, i, 0)),
        out_shape=jax.ShapeDtypeStruct((b, nb * tq, SWA_WIDTH), BF16),
        compiler_params=_params(("parallel", "parallel")),
        name="swa",
    )(*_swa_tables(slopes, sinks, tq), q_arr, *kv_arrs)


def _res_tile(acts, ws, glu, x_ref, gpost_ref, gnext_ref, xo_ref, ho_ref):
    y = jnp.dot(acts[0][...], ws[0][...], preferred_element_type=F32)
    for a, w in zip(acts[1:], ws[1:]):
        y = y + jnp.dot(a[...], w[...], preferred_element_type=F32)
    if glu:
        y = y[:, :D_MODEL] * jax.nn.sigmoid(y[:, D_MODEL:])
    x_new = x_ref[...] + _rms(y, gpost_ref[...])
    xo_ref[...] = x_new
    if ho_ref is not None:
        ho_ref[...] = _rms(x_new, gnext_ref[...]).astype(ho_ref.dtype)


def _proj_res_kernel(*refs, n_pieces, glu, has_next, has_cast):
    it = iter(refs)
    take = lambda k: [next(it) for _ in range(k)]
    acts, acts_s, ws = take(n_pieces), take(n_pieces), take(n_pieces)
    x_ref, xs_ref, gpost_ref = take(3)
    gnext_ref = next(it) if has_next else None
    src_ref = next(it) if has_cast else None
    xo_ref, xos_ref = take(2)
    ho_ref, hos_ref = take(2) if has_next else (None, None)
    dst_ref = next(it) if has_cast else None

    _res_tile(acts, ws, glu, x_ref, gpost_ref, gnext_ref, xo_ref, ho_ref)

    @pl.when(pl.program_id(0) == 0)
    def _():
        _res_tile(acts_s, ws, glu, xs_ref, gpost_ref, gnext_ref, xos_ref, hos_ref)

    if has_cast:
        dst_ref[...] = src_ref[...].astype(dst_ref.dtype)


def proj_residual(acts, acts_s, w, w_row_blocks, glu, x, xs, g_post, g_next, tm, name, cast=None):
    m, ms = x.shape[0], xs.shape[0]
    steps = m // tm
    n_pieces = len(acts)
    has_next = g_next is not None
    row = lambda i: (i, 0)
    fixed = lambda i: (0, 0)
    in_specs = [pl.BlockSpec((tm, a.shape[1]), row) for a in acts]
    in_specs += [pl.BlockSpec((ms, a.shape[1]), fixed) for a in acts_s]
    args = list(acts) + list(acts_s)
    for p in range(n_pieces):
        in_specs.append(_resident((acts[p].shape[1], w.shape[1]),
                                  functools.partial(lambda i, rb: (rb, 0), rb=w_row_blocks[p])))
        args.append(w)
    in_specs += [pl.BlockSpec((tm, D_MODEL), row), pl.BlockSpec((ms, D_MODEL), fixed), pl.BlockSpec((1, D_MODEL), fixed)]
    args += [x, xs, g_post.reshape(1, D_MODEL)]
    if has_next:
        in_specs.append(pl.BlockSpec((1, D_MODEL), fixed))
        args.append(g_next.reshape(1, D_MODEL))
    if cast is not None:
        src, li = cast
        slab = src.shape[1] // steps
        in_specs.append(pl.BlockSpec((None, slab, src.shape[2]), lambda i: (li, i, 0)))
        args.append(src)
    out_specs = [pl.BlockSpec((tm, D_MODEL), row), pl.BlockSpec((ms, D_MODEL), fixed)]
    out_shape = [jax.ShapeDtypeStruct((m, D_MODEL), F32), jax.ShapeDtypeStruct((ms, D_MODEL), F32)]
    if has_next:
        out_specs += [pl.BlockSpec((tm, D_MODEL), row), pl.BlockSpec((ms, D_MODEL), fixed)]
        out_shape += [jax.ShapeDtypeStruct((m, D_MODEL), BF16), jax.ShapeDtypeStruct((ms, D_MODEL), BF16)]
    if cast is not None:
        out_specs.append(pl.BlockSpec((slab, src.shape[2]), row))
        out_shape.append(jax.ShapeDtypeStruct(src.shape[1:], BF16))
    outs = pl.pallas_call(
        functools.partial(_proj_res_kernel, n_pieces=n_pieces, glu=glu, has_next=has_next, has_cast=cast is not None),
        grid=(steps,),
        in_specs=in_specs,
        out_specs=out_specs,
        out_shape=out_shape,
        compiler_params=_params(("arbitrary",)),
        name=name,
    )(*args)
    outs = list(outs)
    x_new, xs_new = outs[0], outs[1]
    h_new, hs_new = (outs[2], outs[3]) if has_next else (None, None)
    w_cast = outs[-1] if cast is not None else None
    return x_new, xs_new, h_new, hs_new, w_cast


def _ffn_rows(x_ref, w_ref, cw_ref, cb_ref, carry_ref, act_ref, *, rs):
    tf = act_ref.shape[1]
    ag = jnp.dot(x_ref[...], w_ref[...], preferred_element_type=F32)
    a, g = ag[:, :tf], ag[:, tf:]
    cr = carry_ref.shape[0]
    ext = jnp.concatenate([carry_ref[...], a], axis=0)
    conv = cb_ref[...] + cw_ref[0:1, :] * pltpu.roll(ext, 2 * rs, 0)[cr:]
    conv = conv + cw_ref[1:2, :] * pltpu.roll(ext, rs, 0)[cr:]
    conv = conv + cw_ref[2:3, :] * a
    act_ref[...] = (jax.nn.gelu(conv) * g).astype(act_ref.dtype)
    carry_ref[...] = ext[a.shape[0]:]


def _ffn_up_kernel(x_ref, xs_ref, wa_ref, wg_ref, wd_ref, cw_ref, cb_ref, init_s_ref,
                   act_ref, act_s_ref, tail_ref, tail_s_ref, wdb_ref, w_ref, carry_ref, carry_s_ref, *,
                   tiles_per_seq, rs_s):
    i = pl.program_id(1)
    t_in_seq = i % tiles_per_seq
    tile = functools.partial(_ffn_rows, w_ref=w_ref, cw_ref=cw_ref, cb_ref=cb_ref)
    tf = act_ref.shape[1]

    @pl.when(i == 0)
    def _():
        w_ref[:, :tf] = wa_ref[...].astype(BF16)
        w_ref[:, tf:] = wg_ref[...].astype(BF16)
        wdb_ref[...] = wd_ref[...].astype(BF16)
        carry_s_ref[...] = init_s_ref[...]
        tile(xs_ref, carry_ref=carry_s_ref, act_ref=act_s_ref, rs=rs_s)
        tail_s_ref[...] = carry_s_ref[...]

    @pl.when(t_in_seq == 0)
    def _():
        carry_ref[...] = jnp.zeros(carry_ref.shape, F32)

    tile(x_ref, carry_ref=carry_ref, act_ref=act_ref, rs=1)

    @pl.when(t_in_seq == tiles_per_seq - 1)
    def _():
        tail_ref[0] = carry_ref[...]


def ffn_up(h, hs, wa, wg, wd, li, conv_w, conv_b, init_s, rs_s, tm, tf, rows_per_seq):
    m, ms = h.shape[0], hs.shape[0]
    cr = SUBLANES
    cr_s = init_s.shape[0]
    tiles_per_seq = rows_per_seq // tm
    nseq = m // rows_per_seq
    n_tiles = D_FF // tf
    slab = D_FF // n_tiles
    col = lambda f, i: (0, f)
    return pl.pallas_call(
        functools.partial(_ffn_up_kernel, tiles_per_seq=tiles_per_seq, rs_s=rs_s),
        grid=(n_tiles, m // tm),
        in_specs=[
            pl.BlockSpec((tm, D_MODEL), lambda f, i: (i, 0)),
            pl.BlockSpec((ms, D_MODEL), lambda f, i: (0, 0)),
            pl.BlockSpec((None, D_MODEL, tf), lambda f, i: (li, 0, f)),
            pl.BlockSpec((None, D_MODEL, tf), lambda f, i: (li, 0, f)),
            pl.BlockSpec((None, slab, D_MODEL), lambda f, i: (li, f, 0)),
            pl.BlockSpec((None, CONV_W, tf), lambda f, i: (li, 0, f)),
            pl.BlockSpec((None, 1, tf), lambda f, i: (li, 0, f)),
            pl.BlockSpec((cr_s, tf), col),
        ],
        out_specs=[pl.BlockSpec((tm, tf), lambda f, i: (i, f)),
                   pl.BlockSpec((ms, tf), col),
                   pl.BlockSpec((1, cr, tf), lambda f, i: (i // tiles_per_seq, 0, f)),
                   pl.BlockSpec((cr_s, tf), col),
                   pl.BlockSpec((slab, D_MODEL), lambda f, i: (f, 0))],
        out_shape=[jax.ShapeDtypeStruct((m, D_FF), BF16), jax.ShapeDtypeStruct((ms, D_FF), BF16),
                   jax.ShapeDtypeStruct((nseq, cr, D_FF), F32), jax.ShapeDtypeStruct((cr_s, D_FF), F32),
                   jax.ShapeDtypeStruct((D_FF, D_MODEL), BF16)],
        scratch_shapes=[pltpu.VMEM((D_MODEL, 2 * tf), BF16), pltpu.VMEM((cr, tf), F32), pltpu.VMEM((cr_s, tf), F32)],
        compiler_params=_params(("arbitrary", "arbitrary")),
        name="ffn_up",
    )(h, hs, wa, wg, wd, conv_w, conv_b.reshape(DEPTH, 1, D_FF), init_s)


def _s5_prep_kernel(lam_ref, lam_x_ref, b_ref, c_ref, lbr_ref, lbi_ref, bdr_ref, bdi_ref, cdr_ref, cdi_ref):
    def lam_bar(lre, lim, log_step):
        delta = jnp.exp(log_step)
        mag = jnp.exp(lre * delta)
        return mag * jnp.cos(lim * delta), mag * jnp.sin(lim * delta)

    lbr, lbi = lam_bar(lam_ref[0], lam_ref[1], lam_ref[2])
    lbr_ref[...] = lbr
    lbi_ref[...] = lbi
    lre, lim = lam_x_ref[0], lam_x_ref[1]
    xr, xi = lam_bar(lre, lim, lam_x_ref[2])
    nr, ni = xr - 1.0, xi
    den = lre * lre + lim * lim
    cr = (nr * lre + ni * lim) / den
    ci = (ni * lre - nr * lim) / den
    br, bi = b_ref[0], b_ref[1]

    def spread(w, group_rows, group_cols):
        rows, width = w.shape
        cols = (rows // group_rows) * group_cols
        idx = lambda shp, d: lax.broadcasted_iota(jnp.int32, shp, d)
        repeat = (idx((width, cols), 1) % width == idx((width, cols), 0)).astype(BF16)
        own = idx((rows, cols), 0) // group_rows == idx((rows, cols), 1) // group_cols
        tiled = jnp.dot(w.astype(BF16), repeat, preferred_element_type=F32)
        return jnp.where(own, tiled, 0.0).astype(BF16)

    bdr_ref[0] = spread(cr * br - ci * bi, SSM_GROUP_CH, SSM_P)
    bdi_ref[0] = spread(cr * bi + ci * br, SSM_GROUP_CH, SSM_P)
    cdr_ref[0] = spread(c_ref[0], SSM_P, SSM_GROUP_CH)
    cdi_ref[0] = spread(c_ref[1], SSM_P, SSM_GROUP_CH)


def s5_prep(lam_re, lam_im, log_step, b_re, b_im, c_re, c_im):
    gpb = SSM_GROUPS // SSM_BLOCKS
    lam = jnp.stack([lam_re, lam_im, jnp.broadcast_to(log_step[:, None], (SSM_GROUPS, SSM_P))])
    lam_x = jnp.repeat(lam, SSM_GROUP_CH, axis=1)
    b = jnp.stack([b_re, b_im]).transpose(0, 1, 3, 2).reshape(2, SSM_GROUPS * SSM_GROUP_CH, SSM_P)
    c = jnp.stack([c_re, c_im]).transpose(0, 1, 3, 2).reshape(2, SSM_GROUPS * SSM_P, SSM_GROUP_CH)
    blk = lambda rows, width: pl.BlockSpec((rows, width), lambda k: (k, 0))
    stacked = lambda n, rows, width: pl.BlockSpec((n, rows, width), lambda k: (0, k, 0))
    mat = lambda rows, cols: pl.BlockSpec((1, rows, cols), lambda k: (k, 0, 0))
    return pl.pallas_call(
        _s5_prep_kernel,
        grid=(SSM_BLOCKS,),
        in_specs=[stacked(3, gpb, SSM_P), stacked(3, SSM_CH_BLOCK, SSM_P), stacked(2, SSM_CH_BLOCK, SSM_P),
                  stacked(2, SSM_LANE_BLOCK, SSM_GROUP_CH)],
        out_specs=[blk(gpb, SSM_P), blk(gpb, SSM_P),
                   mat(SSM_CH_BLOCK, SSM_LANE_BLOCK), mat(SSM_CH_BLOCK, SSM_LANE_BLOCK),
                   mat(SSM_LANE_BLOCK, SSM_CH_BLOCK), mat(SSM_LANE_BLOCK, SSM_CH_BLOCK)],
        out_shape=[jax.ShapeDtypeStruct((SSM_GROUPS, SSM_P), F32)] * 2
        + [jax.ShapeDtypeStruct((SSM_BLOCKS, SSM_CH_BLOCK, SSM_LANE_BLOCK), BF16)] * 2
        + [jax.ShapeDtypeStruct((SSM_BLOCKS, SSM_LANE_BLOCK, SSM_CH_BLOCK), BF16)] * 2,
        compiler_params=_params(("parallel",)),
        name="s5_prep",
    )(lam, lam_x, b, c)


def _s5_dense_kernel(x_ref, gpre_ref, d_ref, lbr_ref, lbi_ref, bdr_ref, bdi_ref, cdr_ref, cdi_ref, s0r_ref, s0i_ref,
                     z_ref, sr_ref, si_ref, ur_ref, ui_ref, *, nb, tt):
    h = _rms(x_ref[...], gpre_ref[...])
    hb = h.astype(BF16)
    for kb in range(SSM_BLOCKS):
        ch = slice(kb * SSM_CH_BLOCK, (kb + 1) * SSM_CH_BLOCK)
        ln = slice(kb * SSM_LANE_BLOCK, (kb + 1) * SSM_LANE_BLOCK)
        ur_ref[...] = jnp.dot(hb[:, ch], bdr_ref[kb], preferred_element_type=F32)
        ui_ref[...] = jnp.dot(hb[:, ch], bdi_ref[kb], preferred_element_type=F32)
        a_re = jnp.broadcast_to(lbr_ref[:, ln], (nb, SSM_LANE_BLOCK))
        a_im = jnp.broadcast_to(lbi_ref[:, ln], (nb, SSM_LANE_BLOCK))
        s_re, s_im = s0r_ref[:, ln], s0i_ref[:, ln]
        for t in range(tt):
            rows = slice(t * nb, (t + 1) * nb)
            s_re, s_im = ((a_re * s_re - a_im * s_im) + ur_ref[rows, :],
                          (a_re * s_im + a_im * s_re) + ui_ref[rows, :])
            ur_ref[rows, :] = s_re
            ui_ref[rows, :] = s_im
        sr_ref[:, ln] = s_re
        si_ref[:, ln] = s_im
        y = (jnp.dot(ur_ref[...].astype(BF16), cdr_ref[kb], preferred_element_type=F32)
             - jnp.dot(ui_ref[...].astype(BF16), cdi_ref[kb], preferred_element_type=F32))
        y = y + d_ref[:, ch] * h[:, ch]
        z_ref[:, ch] = jax.nn.gelu(y).astype(z_ref.dtype)


def s5_dense(x, g_pre, d_skip, lbr, lbi, bdr, bdi, cdr, cdi, s0r, s0i, nb, tt):
    rows = nb * tt
    full = lambda shp: pl.BlockSpec(shp, lambda: tuple(0 for _ in shp))
    bd = full((SSM_BLOCKS, SSM_CH_BLOCK, SSM_LANE_BLOCK))
    cd = full((SSM_BLOCKS, SSM_LANE_BLOCK, SSM_CH_BLOCK))
    return pl.pallas_call(
        functools.partial(_s5_dense_kernel, nb=nb, tt=tt),
        in_specs=[full((rows, D_MODEL)), full((1, D_MODEL)), full((1, D_MODEL)), full((1, SSM_STATE)),
                  full((1, SSM_STATE)), bd, bd, cd, cd, full((nb, SSM_STATE)), full((nb, SSM_STATE))],
        out_specs=[full((rows, D_MODEL)), full((nb, SSM_STATE)), full((nb, SSM_STATE))],
        out_shape=[jax.ShapeDtypeStruct((rows, D_MODEL), BF16),
                   jax.ShapeDtypeStruct((nb, SSM_STATE), F32), jax.ShapeDtypeStruct((nb, SSM_STATE), F32)],
        scratch_shapes=[pltpu.VMEM((rows, SSM_LANE_BLOCK), F32), pltpu.VMEM((rows, SSM_LANE_BLOCK), F32)],
        compiler_params=pltpu.CompilerParams(vmem_limit_bytes=VMEM_LIMIT),
        name="s5_dense",
    )(x, g_pre.reshape(1, D_MODEL), d_skip.reshape(1, D_MODEL), lbr.reshape(1, SSM_STATE), lbi.reshape(1, SSM_STATE),
      bdr, bdi, cdr, cdi, s0r, s0i)


def _s5_seq_kernel(x_ref, gpre_ref, d_ref, lam_r_ref, lam_i_ref, bdr_ref, bdi_ref, cdr_ref, cdi_ref,
                   z_ref, sr_ref, si_ref, ur_ref, ui_ref, hs_ref, yo_ref):
    tt, pitch = S5_TT, S5_PITCH
    ti = pl.program_id(1)

    @pl.when(ti == 0)
    def _():
        sr_ref[...] = jnp.zeros(sr_ref.shape, F32)
        si_ref[...] = jnp.zeros(si_ref.shape, F32)

    h = _rms(x_ref[...], gpre_ref[...])
    zeros8 = jnp.zeros((SUBLANES, D_MODEL), F32)
    hs_ref[0:SUBLANES, :] = zeros8
    hs_ref[SUBLANES + tt:2 * SUBLANES + tt, :] = zeros8
    hs_ref[SUBLANES:SUBLANES + tt, :] = h
    hb = h.astype(BF16)
    hb_shift = hs_ref[4:tt + 12, :].astype(BF16)

    def even_rows(kb, lt):
        return pl.ds((kb * SUBLANES + 2 * lt) * pitch, tt)

    def odd_rows(kb, lt):
        return pl.ds((kb * SUBLANES + 2 * lt + 1) * pitch - 4, tt + SUBLANES)

    for kb in range(SSM_BLOCKS):
        ch = slice(kb * SSM_CH_BLOCK, (kb + 1) * SSM_CH_BLOCK)
        for u_ref, bd_ref in ((ur_ref, bdr_ref), (ui_ref, bdi_ref)):
            ue = jnp.dot(hb[:, ch], bd_ref[kb, :, 0:SSM_HALF], preferred_element_type=F32)
            uo = jnp.dot(hb_shift[:, ch], bd_ref[kb, :, SSM_HALF:SSM_LANE_BLOCK], preferred_element_type=F32)
            for lt in range(4):
                u_ref[even_rows(kb, lt), :] = ue[:, lt * LANES:(lt + 1) * LANES]
                u_ref[odd_rows(kb, lt), :] = uo[:, lt * LANES:(lt + 1) * LANES]

    blk = lambda q: slice(q * SUBLANES, (q + 1) * SUBLANES)
    a_re = [lam_r_ref[blk(q), :] for q in range(SSM_BLOCKS)]
    a_im = [lam_i_ref[blk(q), :] for q in range(SSM_BLOCKS)]

    def step(t, carry):
        new = []
        for q in range(SSM_BLOCKS):
            s_re, s_im = carry[2 * q], carry[2 * q + 1]
            rows = pl.ds(q * SUBLANES * pitch + t, SUBLANES, stride=pitch)
            n_re = (a_re[q] * s_re - a_im[q] * s_im) + ur_ref[rows, :]
            n_im = (a_re[q] * s_im + a_im[q] * s_re) + ui_ref[rows, :]
            ur_ref[rows, :] = n_re
            ui_ref[rows, :] = n_im
            new += [n_re, n_im]
        return tuple(new)

    init = []
    for q in range(SSM_BLOCKS):
        init += [sr_ref[0, blk(q), :], si_ref[0, blk(q), :]]
    final = lax.fori_loop(0, tt, step, tuple(init), unroll=2)
    for q in range(SSM_BLOCKS):
        sr_ref[0, blk(q), :] = final[2 * q]
        si_ref[0, blk(q), :] = final[2 * q + 1]

    for kb in range(SSM_BLOCKS):
        ch = slice(kb * SSM_CH_BLOCK, (kb + 1) * SSM_CH_BLOCK)

        def gather(u_ref, rows_of):
            return jnp.concatenate([u_ref[rows_of(kb, lt), :] for lt in range(4)], axis=1).astype(BF16)

        y_even = (jnp.dot(gather(ur_ref, even_rows), cdr_ref[kb, 0:SSM_HALF, :], preferred_element_type=F32)
                  - jnp.dot(gather(ui_ref, even_rows), cdi_ref[kb, 0:SSM_HALF, :], preferred_element_type=F32))
        yo_ref[...] = (jnp.dot(gather(ur_ref, odd_rows), cdr_ref[kb, SSM_HALF:SSM_LANE_BLOCK, :],
                               preferred_element_type=F32)
                       - jnp.dot(gather(ui_ref, odd_rows), cdi_ref[kb, SSM_HALF:SSM_LANE_BLOCK, :],
                                 preferred_element_type=F32))
        y = y_even + yo_ref[4:tt + 4, :]
        y = y + d_ref[:, ch] * h[:, ch]
        z_ref[:, ch] = jax.nn.gelu(y).astype(z_ref.dtype)


def s5_seq(x, g_pre, d_skip, lam_r, lam_i, bdr, bdi, cdr, cdi, nseq):
    m = x.shape[0]
    ntt = m // (nseq * S5_TT)
    fixed2 = lambda shp: pl.BlockSpec(shp, lambda si, ti: (0, 0))
    state = pl.BlockSpec((1, SSM_SLABS, LANES), lambda si, ti: (si, 0, 0))
    bd = _resident((SSM_BLOCKS, SSM_CH_BLOCK, SSM_LANE_BLOCK), lambda si, ti: (0, 0, 0))
    cd = _resident((SSM_BLOCKS, SSM_LANE_BLOCK, SSM_CH_BLOCK), lambda si, ti: (0, 0, 0))
    return pl.pallas_call(
        _s5_seq_kernel,
        grid=(nseq, ntt),
        in_specs=[pl.BlockSpec((S5_TT, D_MODEL), lambda si, ti: (si * ntt + ti, 0)),
                  fixed2((1, D_MODEL)), fixed2((1, D_MODEL)), fixed2((SSM_SLABS, LANES)), fixed2((SSM_SLABS, LANES)),
                  bd, bd, cd, cd],
        out_specs=[pl.BlockSpec((S5_TT, D_MODEL), lambda si, ti: (si * ntt + ti, 0)), state, state],
        out_shape=[jax.ShapeDtypeStruct((m, D_MODEL), BF16),
                   jax.ShapeDtypeStruct((nseq, SSM_SLABS, LANES), F32),
                   jax.ShapeDtypeStruct((nseq, SSM_SLABS, LANES), F32)],
        scratch_shapes=[pltpu.VMEM((SSM_SLABS * S5_PITCH, LANES), F32), pltpu.VMEM((SSM_SLABS * S5_PITCH, LANES), F32),
                        pltpu.VMEM((S5_TT + 2 * SUBLANES, D_MODEL), F32),
                        pltpu.VMEM((S5_TT + SUBLANES, SSM_CH_BLOCK), F32)],
        compiler_params=_params(("parallel", "arbitrary")),
        name="s5_seq",
    )(x, g_pre.reshape(1, D_MODEL), d_skip.reshape(1, D_MODEL), lam_r, lam_i, bdr, bdi, cdr, cdi)


def _to_slabs(v):
    lead = v.shape[:-1]
    v = v.reshape(lead + (SSM_BLOCKS, SUBLANES, LANES))
    return jnp.take(v, jnp.array(SLAB_LANE_TILE), axis=-2).reshape(lead + (SSM_SLABS, LANES))


def _from_slabs(s):
    lead = s.shape[:-2]
    s = s.reshape(lead + (SSM_BLOCKS, SUBLANES, LANES))
    return jnp.take(s, jnp.array(LANE_TILE_SLAB), axis=-2).reshape(lead + (SSM_STATE,))


def _slopes():
    return 2.0 ** (-8.0 * jnp.arange(1, SWA_Q_HEADS + 1, dtype=F32) / SWA_Q_HEADS)


def kernel(x_prompt, x_sample, state_ret, cache_swa_k, cache_swa_v, state_ssm_re, state_ssm_im, state_ffn_conv, norm_mix_pre, norm_mix_post, norm_ffn_pre, norm_ffn_post, w_in_even, w_out_even, swa_sinks, ssm_lam_re, ssm_lam_im, ssm_log_step, ssm_b_re, ssm_b_im, ssm_c_re, ssm_c_im, ssm_d, w_glu, ffn_w_a, ffn_w_g, ffn_conv_w, ffn_conv_b, ffn_w_down):
    pb, pl_len, _ = x_prompt.shape
    sb, sl_len, _ = x_sample.shape
    mp = pb * pl_len
    ms = sb * sl_len
    s_pad = BF16_ROWS
    carry_rows_s = (CONV_W - 1) * sb
    slopes = _slopes()

    xp = x_prompt.reshape(mp, D_MODEL)
    xs = x_sample.transpose(1, 0, 2).reshape(ms, D_MODEL)

    hp = rmsnorm_bf16(xp, norm_mix_pre[0], 512)
    hs = rmsnorm_bf16(xs, norm_mix_pre[0], ms)

    zero_ret = jnp.zeros((1, pb, RET_HEADS, RET_DK, RET_DV), F32)
    ret_p, ret_s, wk_p, wk_s, wv_p, wv_s = [], [], [], [], [], []
    sre_p, sre_s, sim_p, sim_s, conv_p, conv_s = [], [], [], [], [], []
    w_glu_b = None

    for layer in range(DEPTH):
        i = layer // 2
        if layer % 2 == 0:
            proj, proj_s, w_out_b = in_proj(hp, hs, w_in_even, w_out_even, i, 1024, 768)
            proj = proj.reshape(pb, pl_len, EVEN_IN)
            o_ret, s_ret = retention(proj, zero_ret, 0, RET_CHUNK, RET_CHUNK, 1, RET_CHUNKS_PER_STEP)
            ret_p.append(s_ret)
            kcol, vcol = KV_COL // LANES, KV_COL // LANES + 1
            blk = lambda c, prev: pl.BlockSpec(
                (1, SWA_BLOCK, LANES),
                (lambda bi, qi: (bi, jnp.maximum(qi - 1, 0), c)) if prev else (lambda bi, qi: (bi, qi, c)))
            o_swa = swa_attention(
                slopes, swa_sinks[i], proj,
                pl.BlockSpec((1, SWA_BLOCK, SWA_WIDTH), lambda bi, qi: (bi, qi, Q_COL // SWA_WIDTH)),
                [proj, proj, proj, proj], [blk(kcol, True), blk(vcol, True), blk(kcol, False), blk(vcol, False)],
                pb, 1, 1, pl_len // SWA_BLOCK, SWA_BLOCK, 0, SWA_BLOCK)
            kv_shape = (pb, WINDOW, SWA_KV_HEADS, SWA_HD)
            wk_p.append(proj[:, pl_len - WINDOW:, KV_COL:KV_COL + KV_WIDTH].reshape(kv_shape))
            wv_p.append(proj[:, pl_len - WINDOW:, KV_COL + KV_WIDTH:KV_COL + 2 * KV_WIDTH].reshape(kv_shape))

            proj_bt = proj_s.reshape(sl_len, sb, EVEN_IN).transpose(1, 0, 2)
            proj_pad = jnp.pad(proj_bt, ((0, 0), (0, s_pad - sl_len), (0, 0)))
            o_ret_s, s_ret_s = retention(proj_pad, state_ret, i, sl_len, s_pad, SAMPLE_SEQS_PER_STEP, 1)
            ret_s.append(s_ret_s)
            win = cache_swa_k.shape[2]
            n_even = cache_swa_k.shape[0]
            ck = cache_swa_k.reshape(n_even, sb, win, KV_WIDTH)
            cv = cache_swa_v.reshape(n_even, sb, win, KV_WIDTH)
            cache_spec = pl.BlockSpec((None, SAMPLE_SEQS_PER_STEP, win, KV_WIDTH), lambda bi, qi: (i, bi, 0, 0))
            cur = lambda c: pl.BlockSpec((SAMPLE_SEQS_PER_STEP, s_pad, LANES), lambda bi, qi: (bi, 0, c))
            o_swa_s = swa_attention(
                slopes, swa_sinks[i], proj_pad,
                pl.BlockSpec((SAMPLE_SEQS_PER_STEP, s_pad, SWA_WIDTH), lambda bi, qi: (bi, 0, Q_COL // SWA_WIDTH)),
                [ck, cv, proj_pad, proj_pad], [cache_spec, cache_spec, cur(kcol), cur(vcol)],
                sb, SAMPLE_SEQS_PER_STEP, SWA_CHAINS, 1, s_pad, win, 0)
            k_new = proj_bt[:, :, KV_COL:KV_COL + KV_WIDTH]
            v_new = proj_bt[:, :, KV_COL + KV_WIDTH:KV_COL + 2 * KV_WIDTH]
            kv_shape_s = (sb, win, SWA_KV_HEADS, SWA_HD)
            wk_s.append(jnp.concatenate([ck[i, :, sl_len:], k_new], axis=1).reshape(kv_shape_s))
            wv_s.append(jnp.concatenate([cv[i, :, sl_len:], v_new], axis=1).reshape(kv_shape_s))
            tb = lambda o: o[:, :sl_len].transpose(1, 0, 2).reshape(ms, -1)

            xp, xs, hp, hs, _ = proj_residual(
                [o_ret.reshape(mp, RET_WIDTH), o_swa.reshape(mp, SWA_WIDTH)], [tb(o_ret_s), tb(o_swa_s)],
                w_out_b, [0, 1], False, xp, xs, norm_mix_post[layer], norm_ffn_pre[layer], 512, "out_proj")
        else:
            lbr, lbi, bdr, bdi, cdr, cdi = s5_prep(ssm_lam_re[i], ssm_lam_im[i], ssm_log_step[i], ssm_b_re[i],
                                                   ssm_b_im[i], ssm_c_re[i], ssm_c_im[i])
            lam_r = _to_slabs(lbr.reshape(SSM_STATE))
            lam_i = _to_slabs(lbi.reshape(SSM_STATE))
            zp, s_re, s_im = s5_seq(xp, norm_mix_pre[layer], ssm_d[i], lam_r, lam_i, bdr, bdi, cdr, cdi, pb)
            sre_p.append(_from_slabs(s_re).reshape(pb, SSM_GROUPS, SSM_P))
            sim_p.append(_from_slabs(s_im).reshape(pb, SSM_GROUPS, SSM_P))
            zs, s_re_s, s_im_s = s5_dense(xs, norm_mix_pre[layer], ssm_d[i], lbr, lbi, bdr, bdi, cdr, cdi,
                                          state_ssm_re[i].reshape(sb, SSM_STATE),
                                          state_ssm_im[i].reshape(sb, SSM_STATE), sb, sl_len)
            sre_s.append(s_re_s.reshape(sb, SSM_GROUPS, SSM_P))
            sim_s.append(s_im_s.reshape(sb, SSM_GROUPS, SSM_P))
            xp, xs, hp, hs, _ = proj_residual([zp], [zs], w_glu_b, [0], True, xp, xs, norm_mix_post[layer],
                                              norm_ffn_pre[layer], 256, "glu_proj")

        init_s = state_ffn_conv[layer].transpose(1, 0, 2).reshape(carry_rows_s, D_FF)
        act, act_s, tail, tail_s, w_down_b = ffn_up(hp, hs, ffn_w_a, ffn_w_g, ffn_w_down, layer, ffn_conv_w,
                                                    ffn_conv_b, init_s, sb, 1024, 512, pl_len)
        conv_p.append(tail[:, SUBLANES - (CONV_W - 1):])
        conv_s.append(tail_s.reshape(CONV_W - 1, sb, D_FF).transpose(1, 0, 2))
        next_even = layer + 1 < DEPTH and (layer + 1) % 2 == 0
        next_odd = layer + 1 < DEPTH and (layer + 1) % 2 == 1
        xp, xs, hp, hs, w_cast = proj_residual(
            [act], [act_s], w_down_b, [0], False, xp, xs, norm_ffn_post[layer],
            norm_mix_pre[layer + 1] if next_even else None, 256, "ffn_down",
            cast=(w_glu, (layer + 1) // 2) if next_odd else None)
        if next_odd:
            w_glu_b = w_cast

    y_prompt = xp.reshape(pb, pl_len, D_MODEL)
    y_sample = xs.reshape(sl_len, sb, D_MODEL).transpose(1, 0, 2)
    return (y_prompt, y_sample, jnp.stack(ret_p), jnp.stack(ret_s), jnp.stack(wk_p), jnp.stack(wk_s),
            jnp.stack(wv_p), jnp.stack(wv_s), jnp.stack(sre_p), jnp.stack(sre_s), jnp.stack(sim_p),
            jnp.stack(sim_s), jnp.stack(conv_p), jnp.stack(conv_s))
```

```python
import functools

import jax
import jax.numpy as jnp
from jax import lax
from jax.experimental import pallas as pl
from jax.experimental.pallas import tpu as pltpu

F32 = jnp.float32
BF16 = jnp.bfloat16

D_MODEL = 2048
DEPTH = 4
RET_HEADS = 8
RET_DK = 128
RET_DV = 128
RET_CHUNK = 128
RET_WIDTH = RET_HEADS * RET_DV
SWA_Q_HEADS = 16
SWA_KV_HEADS = 2
SWA_HD = 64
WINDOW = 128
SWA_BLOCK = 128
SWA_WIDTH = SWA_Q_HEADS * SWA_HD
SWA_CHAINS = 4
RET_CHUNKS_PER_STEP = 4
RES_ROW_PARTS = 2
FFN_TF = 512
SAMPLE_SEQS_PER_STEP = 4
EVEN_IN = 5376
Q_COL = RET_WIDTH * 4
KV_COL = Q_COL + SWA_WIDTH
KV_WIDTH = SWA_KV_HEADS * SWA_HD
SSM_GROUP_CH = 16
SSM_GROUPS = 128
SSM_P = 64
SSM_STATE = SSM_GROUPS * SSM_P
SSM_LANE_BLOCK = 1024
SSM_CH_BLOCK = 256
SSM_BLOCKS = SSM_STATE // SSM_LANE_BLOCK
SSM_HALF = SSM_LANE_BLOCK // 2
D_FF = 5632
CONV_W = 3
NORM_EPS = 1e-6

LANES = 128
SUBLANES = 8
BF16_ROWS = 16
VMEM_LIMIT = 56 * 1024 * 1024

SSM_SLABS = SSM_STATE // LANES
S5_TT = 256
S5_PITCH = S5_TT + 4


def _params(sem, vmem=VMEM_LIMIT):
    return pltpu.CompilerParams(dimension_semantics=sem, vmem_limit_bytes=vmem)


def _rms(x, g):
    return x * lax.rsqrt(jnp.mean(x * x, axis=-1, keepdims=True) + NORM_EPS) * g


def _resident(shape, index_map):
    return pl.BlockSpec(shape, index_map, pipeline_mode=pl.Buffered(1))


def _norm_kernel(x_ref, g_ref, o_ref):
    o_ref[...] = _rms(x_ref[...], g_ref[...]).astype(o_ref.dtype)


def rmsnorm_bf16(x, g, tm):
    m, d = x.shape
    return pl.pallas_call(
        _norm_kernel,
        grid=(m // tm,),
        in_specs=[pl.BlockSpec((tm, d), lambda i: (i, 0)), pl.BlockSpec((1, d), lambda i: (0, 0))],
        out_specs=pl.BlockSpec((tm, d), lambda i: (i, 0)),
        out_shape=jax.ShapeDtypeStruct((m, d), BF16),
        compiler_params=_params(("parallel",)),
        name="rmsnorm",
    )(x, g.reshape(1, d))


def _in_proj_kernel(x_ref, xs_ref, w_ref, wo_ref, o_ref, os_ref, wob_ref, wb_ref):
    @pl.when(pl.program_id(1) == 0)
    def _():
        wb_ref[...] = w_ref[...].astype(BF16)
        wob_ref[...] = wo_ref[...].astype(BF16)
        os_ref[...] = jnp.dot(xs_ref[...], wb_ref[...], preferred_element_type=F32)

    o_ref[...] = jnp.dot(x_ref[...], wb_ref[...], preferred_element_type=F32)


def in_proj(h, hs, w_in, w_out, li, tm, tn):
    m, ms = h.shape[0], hs.shape[0]
    n_tiles = EVEN_IN // tn
    slab = 512
    n_slabs = D_MODEL // slab
    assert n_slabs <= n_tiles
    slab_idx = lambda j, i: (li, jnp.minimum(j, n_slabs - 1), 0)
    return pl.pallas_call(
        _in_proj_kernel,
        grid=(n_tiles, m // tm),
        in_specs=[pl.BlockSpec((tm, D_MODEL), lambda j, i: (i, 0)),
                  pl.BlockSpec((ms, D_MODEL), lambda j, i: (0, 0)),
                  pl.BlockSpec((None, D_MODEL, tn), lambda j, i: (li, 0, j)),
                  pl.BlockSpec((None, slab, D_MODEL), slab_idx)],
        out_specs=[pl.BlockSpec((tm, tn), lambda j, i: (i, j)),
                   pl.BlockSpec((ms, tn), lambda j, i: (0, j)),
                   pl.BlockSpec((slab, D_MODEL), lambda j, i: (jnp.minimum(j, n_slabs - 1), 0))],
        out_shape=[jax.ShapeDtypeStruct((m, EVEN_IN), F32), jax.ShapeDtypeStruct((ms, EVEN_IN), F32),
                   jax.ShapeDtypeStruct((D_MODEL, D_MODEL), BF16)],
        scratch_shapes=[pltpu.VMEM((D_MODEL, tn), BF16)],
        compiler_params=_params(("arbitrary", "arbitrary")),
        name="in_proj",
    )(h, hs, w_in, w_out)


def _retention_kernel(decay_ref, q_ref, k_ref, v_ref, g_ref, intra_ref, read_ref, write_ref, s0_ref, *rest,
                      nseq, chunks, c):
    prev_refs, o_ref, out_ref = rest[:-2], rest[-2], rest[-1]
    s_ref = out_ref.at[len(prev_refs)] if prev_refs else out_ref
    ci = pl.program_id(1)

    @pl.when(ci == 0)
    def _():
        s_ref[...] = s0_ref[...]
        for j, prev_ref in enumerate(prev_refs):
            out_ref[j] = prev_ref[...]

    heads = [slice(h * RET_DK, (h + 1) * RET_DK) for h in range(RET_HEADS)]
    nt = (((1,), (1,)), ((), ()))
    tn = (((0,), (0,)), ((), ()))
    jobs = [(b, h) for b in range(nseq) for h in range(RET_HEADS)]
    for cc in range(chunks):
        rows = slice(cc * c, (cc + 1) * c)
        first = []
        for b, h in jobs:
            sl = heads[h]
            q = q_ref[b, rows, sl].astype(BF16)
            k = k_ref[b, rows, sl] * (RET_DK ** -0.5)
            v = v_ref[b, rows, sl].astype(BF16)
            s = s_ref[b, h]
            sc = lax.dot_general(q, k.astype(BF16), nt, preferred_element_type=F32)
            qs = jnp.dot(q, s.astype(BF16), preferred_element_type=F32)
            kv = lax.dot_general((k * write_ref[h]).astype(BF16), v, tn, preferred_element_type=F32)
            first.append((v, s, sc, qs, kv))
        outs = []
        for (b, h), (v, s, sc, qs, kv) in zip(jobs, first):
            outs.append(jnp.dot((sc * intra_ref[h]).astype(BF16), v, preferred_element_type=F32) + qs * read_ref[h])
            s_ref[b, h] = s * decay_ref[h] + kv
        for (b, h), o in zip(jobs, outs):
            g = g_ref[b, rows, heads[h]]
            o = o * lax.rsqrt(jnp.mean(o * o, axis=-1, keepdims=True) + NORM_EPS)
            o_ref[b, rows, heads[h]] = (o * (g * jax.nn.sigmoid(g))).astype(o_ref.dtype)


def _retention_tables(c_real, c_pad):
    lg = jnp.log(1.0 - 2.0 ** (-5.0 - jnp.arange(RET_HEADS, dtype=F32)))
    idx = jnp.arange(c_pad, dtype=F32)
    diff = idx[:, None] - idx[None, :]
    intra = jnp.where(diff >= 0, jnp.exp(lg[:, None, None] * jnp.maximum(diff, 0.0)), 0.0)
    read = jnp.exp(lg[:, None] * (idx[None, :] + 1.0))
    write = jnp.exp(lg[:, None] * (c_real - 1.0 - idx[None, :]))
    decay = jnp.exp(lg * c_real)
    bshape = (RET_HEADS, c_pad, RET_DV)
    return decay, intra, jnp.broadcast_to(read[:, :, None], bshape), jnp.broadcast_to(write[:, :, None], bshape)


def retention(proj, s0, li, prev_states, c_real, c_pad, nseq, chunks):
    b, l, _ = proj.shape
    rows = chunks * c_pad
    decay, intra, read, write = _retention_tables(c_real, c_pad)

    def col(j):
        return pl.BlockSpec((nseq, rows, RET_WIDTH), lambda bi, ci: (bi, ci, j))

    state_shape = (nseq, RET_HEADS, RET_DK, RET_DV)
    state_spec = pl.BlockSpec(state_shape, lambda bi, ci: (bi, 0, 0, 0))
    out_state_shape = s0.shape[1:]
    out_state_spec = state_spec
    if prev_states:
        n_out = len(prev_states) + 1
        out_state_shape = (n_out,) + out_state_shape
        out_state_spec = pl.BlockSpec((n_out,) + state_shape, lambda bi, ci: (0, bi, 0, 0, 0))
    return pl.pallas_call(
        functools.partial(_retention_kernel, nseq=nseq, chunks=chunks, c=c_pad),
        grid=(b // nseq, l // rows),
        in_specs=[
            pl.BlockSpec(memory_space=pltpu.SMEM),
            col(0), col(1), col(2), col(3),
            pl.BlockSpec((RET_HEADS, c_pad, c_pad), lambda bi, ci: (0, 0, 0)),
            pl.BlockSpec((RET_HEADS, c_pad, RET_DV), lambda bi, ci: (0, 0, 0)),
            pl.BlockSpec((RET_HEADS, c_pad, RET_DV), lambda bi, ci: (0, 0, 0)),
            pl.BlockSpec((None,) + state_shape, lambda bi, ci: (li, bi, 0, 0, 0)),
        ] + [state_spec] * len(prev_states),
        out_specs=[pl.BlockSpec((nseq, rows, RET_WIDTH), lambda bi, ci: (bi, ci, 0)), out_state_spec],
        out_shape=[jax.ShapeDtypeStruct((b, l, RET_WIDTH), BF16), jax.ShapeDtypeStruct(out_state_shape, F32)],
        compiler_params=_params(("parallel", "arbitrary")),
        name="retention",
    )(decay, proj, proj, proj, proj, intra, read, write, s0, *prev_states)


def _swa_kernel(bias_p_ref, bias_c_ref, sink_ref, q_ref, kp_ref, vp_ref, kc_ref, vc_ref, o_ref, *,
                nseq, group, tq, q0_base, q0_step):
    prev_always_visible = q0_step == 0 and q0_base >= WINDOW
    if not prev_always_visible:
        q0 = q0_base + pl.program_id(1) * q0_step
        prev_penalty = jnp.where(q0 >= WINDOW, 0.0, -jnp.inf).astype(F32)
    lane_lo = lax.broadcasted_iota(jnp.int32, (tq, LANES), 1) < SWA_HD
    scale = SWA_HD ** -0.5
    nt = (((1,), (1,)), ((), ()))
    pairs_per_kv = SWA_Q_HEADS // SWA_KV_HEADS // 2
    lane_sum = lambda x: jnp.sum(x, axis=-1, keepdims=True)
    lane_max = lambda x: jnp.max(x, axis=-1, keepdims=True)

    def both(ref, b):
        x = ref[b]
        return x.astype(BF16), pltpu.roll(x, SWA_HD, 1).astype(BF16)

    keys = [(both(kp_ref, b), both(kc_ref, b)) for b in range(nseq)]
    vals = [(both(vp_ref, b), both(vc_ref, b)) for b in range(nseq)]
    all_chains = [(b, kv, parity) for b in range(nseq) for kv in range(SWA_KV_HEADS) for parity in range(2)]
    outs = {}
    for g0 in range(0, len(all_chains), group):
        chains = all_chains[g0:g0 + group]
        scores = []
        for b, kv, parity in chains:
            swapped = (kv == 0) != (parity == 0)
            (kp, kp_sw), (kc, kc_sw) = keys[b]
            kprev, kcur = (kp_sw, kc_sw) if swapped else (kp, kc)
            qs = []
            for p in range(pairs_per_kv):
                pair = kv * pairs_per_kv + p
                qp = q_ref[b, :, pair * LANES:(pair + 1) * LANES]
                qs.append(jnp.where(lane_lo, qp, 0.0) if parity == 0 else jnp.where(lane_lo, 0.0, qp))
            qh = jnp.concatenate(qs, axis=0).astype(BF16)
            scores.append((lax.dot_general(qh, kprev, nt, preferred_element_type=F32),
                           lax.dot_general(qh, kcur, nt, preferred_element_type=F32)))
        probs = []
        for (b, kv, parity), (sp, sc) in zip(chains, scores):
            chain = 2 * kv + parity
            sink = sink_ref[chain][:, :1]
            sp = sp * scale + bias_p_ref[chain]
            sc = sc * scale + bias_c_ref[chain]
            if not prev_always_visible:
                sp = sp + prev_penalty
            if tq == WINDOW:
                m = jnp.maximum(lane_max(jnp.maximum(sp, sc)), sink)
            else:
                m = jnp.maximum(jnp.maximum(lane_max(sp), lane_max(sc)), sink)
            pp = jnp.exp(sp - m)
            pc = jnp.exp(sc - m)
            psum = lane_sum(pp + pc) if tq == WINDOW else lane_sum(pp) + lane_sum(pc)
            denom = psum + jnp.exp(sink - m)
            probs.append(((pp / denom).astype(BF16), (pc / denom).astype(BF16)))
        for (b, kv, parity), (pp, pc) in zip(chains, probs):
            swapped = (kv == 0) != (parity == 0)
            (vp, vp_sw), (vc, vc_sw) = vals[b]
            vprev, vcur = (vp_sw, vc_sw) if swapped else (vp, vc)
            outs[b, kv, parity] = (jnp.dot(pp, vprev, preferred_element_type=F32)
                                   + jnp.dot(pc, vcur, preferred_element_type=F32))
        for b, kv, parity in chains:
            if parity == 0:
                continue
            for p in range(pairs_per_kv):
                pair = kv * pairs_per_kv + p
                rs = slice(p * tq, (p + 1) * tq)
                o_ref[b, :, pair * LANES:(pair + 1) * LANES] = jnp.where(lane_lo, outs[b, kv, 0][rs],
                                                                          outs[b, kv, 1][rs]).astype(o_ref.dtype)


def _swa_tables(slopes, sinks, tq):
    per = SWA_Q_HEADS // SWA_CHAINS

    def rows_of(vals):
        v = vals.astype(F32).reshape(SWA_KV_HEADS, per, 2).transpose(0, 2, 1).reshape(SWA_CHAINS, per)
        return jnp.broadcast_to(v[:, :, None, None], (SWA_CHAINS, per, tq, 1))

    r = jnp.arange(tq, dtype=jnp.int32)[:, None]
    dist_p = r - jnp.arange(WINDOW, dtype=jnp.int32)[None, :] + WINDOW
    dist_c = r - jnp.arange(tq, dtype=jnp.int32)[None, :]
    slope = rows_of(slopes)
    bias_p = jnp.where(dist_p <= WINDOW, -(slope * dist_p.astype(F32)), -jnp.inf)
    bias_c = jnp.where(dist_c >= 0, -(slope * dist_c.astype(F32)), -jnp.inf)
    sink = jnp.broadcast_to(rows_of(sinks), (SWA_CHAINS, per, tq, LANES))
    flat = lambda t: t.reshape(SWA_CHAINS, per * tq, t.shape[-1])
    return flat(bias_p), flat(bias_c), flat(sink)


def swa_attention(slopes, sinks, q_arr, q_spec, kv_arrs, kv_specs, b, nseq, group, nb, tq, q0_base, q0_step):
    assert q0_base % WINDOW == 0 and q0_step % WINDOW == 0
    rows = (SWA_Q_HEADS // SWA_CHAINS) * tq
    table = lambda width: pl.BlockSpec((SWA_CHAINS, rows, width), lambda bi, i: (0, 0, 0))
    return pl.pallas_call(
        functools.partial(_swa_kernel, nseq=nseq, group=group, tq=tq, q0_base=q0_base, q0_step=q0_step),
        grid=(b // nseq, nb),
        in_specs=[table(WINDOW), table(tq), table(LANES), q_spec] + kv_specs,
        out_specs=pl.BlockSpec((nseq, tq, SWA_WIDTH), lambda bi, i: (bi, i, 0)),
        out_shape=jax.ShapeDtypeStruct((b, nb * tq, SWA_WIDTH), BF16),
        compiler_params=_params(("parallel", "parallel")),
        name="swa",
    )(*_swa_tables(slopes, sinks, tq), q_arr, *kv_arrs)


def _res_tile(acts, ws, glu, x_ref, gpost_ref, gnext_ref, xo_ref, ho_ref, parts=1):
    rows = x_ref.shape[0] // parts
    ys = []
    for p in range(parts):
        rs = slice(p * rows, (p + 1) * rows)
        y = jnp.dot(acts[0][rs, :], ws[0][...], preferred_element_type=F32)
        for a, w in zip(acts[1:], ws[1:]):
            y = y + jnp.dot(a[rs, :], w[...], preferred_element_type=F32)
        ys.append(y)
    for p, y in enumerate(ys):
        rs = slice(p * rows, (p + 1) * rows)
        if glu:
            y = y[:, :D_MODEL] * jax.nn.sigmoid(y[:, D_MODEL:])
        x_new = x_ref[rs, :] + _rms(y, gpost_ref[...])
        xo_ref[rs, :] = x_new
        if ho_ref is not None:
            ho_ref[rs, :] = _rms(x_new, gnext_ref[...]).astype(ho_ref.dtype)


def _proj_res_kernel(*refs, n_pieces, glu, has_next, has_cast, parts):
    it = iter(refs)
    take = lambda k: [next(it) for _ in range(k)]
    acts, acts_s, ws = take(n_pieces), take(n_pieces), take(n_pieces)
    x_ref, xs_ref, gpost_ref = take(3)
    gnext_ref = next(it) if has_next else None
    src_ref = next(it) if has_cast else None
    xo_ref, xos_ref = take(2)
    ho_ref, hos_ref = take(2) if has_next else (None, None)
    dst_ref = next(it) if has_cast else None

    _res_tile(acts, ws, glu, x_ref, gpost_ref, gnext_ref, xo_ref, ho_ref, parts=parts)

    @pl.when(pl.program_id(0) == 0)
    def _():
        _res_tile(acts_s, ws, glu, xs_ref, gpost_ref, gnext_ref, xos_ref, hos_ref)

    if has_cast:
        dst_ref[...] = src_ref[...].astype(dst_ref.dtype)


def proj_residual(acts, acts_s, w, w_row_blocks, glu, x, xs, g_post, g_next, tm, parts, name, cast=None):
    m, ms = x.shape[0], xs.shape[0]
    steps = m // tm
    n_pieces = len(acts)
    has_next = g_next is not None
    row = lambda i: (i, 0)
    fixed = lambda i: (0, 0)
    in_specs = [pl.BlockSpec((tm, a.shape[1]), row) for a in acts]
    in_specs += [pl.BlockSpec((ms, a.shape[1]), fixed) for a in acts_s]
    args = list(acts) + list(acts_s)
    for p in range(n_pieces):
        in_specs.append(_resident((acts[p].shape[1], w.shape[1]),
                                  functools.partial(lambda i, rb: (rb, 0), rb=w_row_blocks[p])))
        args.append(w)
    in_specs += [pl.BlockSpec((tm, D_MODEL), row), pl.BlockSpec((ms, D_MODEL), fixed), pl.BlockSpec((1, D_MODEL), fixed)]
    args += [x, xs, g_post.reshape(1, D_MODEL)]
    if has_next:
        in_specs.append(pl.BlockSpec((1, D_MODEL), fixed))
        args.append(g_next.reshape(1, D_MODEL))
    if cast is not None:
        src, li = cast
        slab = src.shape[1] // steps
        in_specs.append(pl.BlockSpec((None, slab, src.shape[2]), lambda i: (li, i, 0)))
        args.append(src)
    out_specs = [pl.BlockSpec((tm, D_MODEL), row), pl.BlockSpec((ms, D_MODEL), fixed)]
    out_shape = [jax.ShapeDtypeStruct((m, D_MODEL), F32), jax.ShapeDtypeStruct((ms, D_MODEL), F32)]
    if has_next:
        out_specs += [pl.BlockSpec((tm, D_MODEL), row), pl.BlockSpec((ms, D_MODEL), fixed)]
        out_shape += [jax.ShapeDtypeStruct((m, D_MODEL), BF16), jax.ShapeDtypeStruct((ms, D_MODEL), BF16)]
    if cast is not None:
        out_specs.append(pl.BlockSpec((slab, src.shape[2]), row))
        out_shape.append(jax.ShapeDtypeStruct(src.shape[1:], BF16))
    outs = pl.pallas_call(
        functools.partial(_proj_res_kernel, n_pieces=n_pieces, glu=glu, has_next=has_next, has_cast=cast is not None,
                          parts=parts),
        grid=(steps,),
        in_specs=in_specs,
        out_specs=out_specs,
        out_shape=out_shape,
        compiler_params=_params(("arbitrary",)),
        name=name,
    )(*args)
    outs = list(outs)
    x_new, xs_new = outs[0], outs[1]
    h_new, hs_new = (outs[2], outs[3]) if has_next else (None, None)
    w_cast = outs[-1] if cast is not None else None
    return x_new, xs_new, h_new, hs_new, w_cast


def _ffn_rows(x_ref, w_ref, cw_ref, cb_ref, carry_ref, act_ref, *, rs):
    tf = act_ref.shape[1]
    ag = jnp.dot(x_ref[...], w_ref[...], preferred_element_type=F32)
    a, g = ag[:, :tf], ag[:, tf:]
    cr = carry_ref.shape[0]
    ext = jnp.concatenate([carry_ref[...], a], axis=0)
    conv = cb_ref[...] + cw_ref[0:1, :] * pltpu.roll(ext, 2 * rs, 0)[cr:]
    conv = conv + cw_ref[1:2, :] * pltpu.roll(ext, rs, 0)[cr:]
    conv = conv + cw_ref[2:3, :] * a
    act_ref[...] = (jax.nn.gelu(conv) * g).astype(act_ref.dtype)
    carry_ref[...] = ext[a.shape[0]:]


def _ffn_up_kernel(x_ref, xs_ref, wa_ref, wg_ref, wd_ref, cw_ref, cb_ref, init_s_ref,
                   act_ref, act_s_ref, tail_ref, tail_s_ref, wdb_ref, w_ref, carry_ref, carry_s_ref, *,
                   tiles_per_seq, rs_s):
    i = pl.program_id(1)
    t_in_seq = i % tiles_per_seq
    tile = functools.partial(_ffn_rows, w_ref=w_ref, cw_ref=cw_ref, cb_ref=cb_ref)
    tf = act_ref.shape[1]

    @pl.when(i == 0)
    def _():
        w_ref[:, :tf] = wa_ref[...].astype(BF16)
        w_ref[:, tf:] = wg_ref[...].astype(BF16)
        wdb_ref[...] = wd_ref[...].astype(BF16)
        carry_s_ref[...] = init_s_ref[...]
        tile(xs_ref, carry_ref=carry_s_ref, act_ref=act_s_ref, rs=rs_s)
        tail_s_ref[...] = carry_s_ref[...]

    @pl.when(t_in_seq == 0)
    def _():
        carry_ref[...] = jnp.zeros(carry_ref.shape, F32)

    tile(x_ref, carry_ref=carry_ref, act_ref=act_ref, rs=1)

    @pl.when(t_in_seq == tiles_per_seq - 1)
    def _():
        tail_ref[0] = carry_ref[...]


def ffn_up(h, hs, wa, wg, wd, li, conv_w, conv_b, init_s, rs_s, tm, tf, rows_per_seq):
    m, ms = h.shape[0], hs.shape[0]
    cr = SUBLANES
    cr_s = init_s.shape[0]
    tiles_per_seq = rows_per_seq // tm
    nseq = m // rows_per_seq
    n_tiles = D_FF // tf
    slab = D_FF // n_tiles
    col = lambda f, i: (0, f)
    return pl.pallas_call(
        functools.partial(_ffn_up_kernel, tiles_per_seq=tiles_per_seq, rs_s=rs_s),
        grid=(n_tiles, m // tm),
        in_specs=[
            pl.BlockSpec((tm, D_MODEL), lambda f, i: (i, 0)),
            pl.BlockSpec((ms, D_MODEL), lambda f, i: (0, 0)),
            pl.BlockSpec((None, D_MODEL, tf), lambda f, i: (li, 0, f)),
            pl.BlockSpec((None, D_MODEL, tf), lambda f, i: (li, 0, f)),
            pl.BlockSpec((None, slab, D_MODEL), lambda f, i: (li, f, 0)),
            pl.BlockSpec((None, CONV_W, tf), lambda f, i: (li, 0, f)),
            pl.BlockSpec((None, 1, tf), lambda f, i: (li, 0, f)),
            pl.BlockSpec((cr_s, tf), col),
        ],
        out_specs=[pl.BlockSpec((tm, tf), lambda f, i: (i, f)),
                   pl.BlockSpec((ms, tf), col),
                   pl.BlockSpec((1, cr, tf), lambda f, i: (i // tiles_per_seq, 0, f)),
                   pl.BlockSpec((cr_s, tf), col),
                   pl.BlockSpec((slab, D_MODEL), lambda f, i: (f, 0))],
        out_shape=[jax.ShapeDtypeStruct((m, D_FF), BF16), jax.ShapeDtypeStruct((ms, D_FF), BF16),
                   jax.ShapeDtypeStruct((nseq, cr, D_FF), F32), jax.ShapeDtypeStruct((cr_s, D_FF), F32),
                   jax.ShapeDtypeStruct((D_FF, D_MODEL), BF16)],
        scratch_shapes=[pltpu.VMEM((D_MODEL, 2 * tf), BF16), pltpu.VMEM((cr, tf), F32), pltpu.VMEM((cr_s, tf), F32)],
        compiler_params=_params(("arbitrary", "arbitrary")),
        name="ffn_up",
    )(h, hs, wa, wg, wd, conv_w, conv_b.reshape(DEPTH, 1, D_FF), init_s)


def _s5_prep_kernel(lam_ref, lam_x_ref, b_ref, c_ref, lbr_ref, lbi_ref, bdr_ref, bdi_ref, cdr_ref, cdi_ref):
    def lam_bar(lre, lim, log_step):
        delta = jnp.exp(log_step)
        mag = jnp.exp(lre * delta)
        return mag * jnp.cos(lim * delta), mag * jnp.sin(lim * delta)

    lbr, lbi = lam_bar(lam_ref[0], lam_ref[1], lam_ref[2])
    lbr_ref[...] = lbr
    lbi_ref[...] = lbi
    lre, lim = lam_x_ref[0], lam_x_ref[1]
    xr, xi = lam_bar(lre, lim, lam_x_ref[2])
    nr, ni = xr - 1.0, xi
    den = lre * lre + lim * lim
    cr = (nr * lre + ni * lim) / den
    ci = (ni * lre - nr * lim) / den
    br, bi = b_ref[0], b_ref[1]

    def spread(w, group_rows, group_cols):
        rows, width = w.shape
        cols = (rows // group_rows) * group_cols
        idx = lambda shp, d: lax.broadcasted_iota(jnp.int32, shp, d)
        repeat = (idx((width, cols), 1) % width == idx((width, cols), 0)).astype(BF16)
        own = idx((rows, cols), 0) // group_rows == idx((rows, cols), 1) // group_cols
        tiled = jnp.dot(w.astype(BF16), repeat, preferred_element_type=F32)
        return jnp.where(own, tiled, 0.0)

    bdr_ref[0] = spread(cr * br - ci * bi, SSM_GROUP_CH, SSM_P).astype(BF16)
    bdi_ref[0] = spread(cr * bi + ci * br, SSM_GROUP_CH, SSM_P).astype(BF16)
    cdr_ref[0] = spread(c_ref[0], SSM_GROUP_CH, SSM_P).T.astype(BF16)
    cdi_ref[0] = spread(c_ref[1], SSM_GROUP_CH, SSM_P).T.astype(BF16)


def s5_prep(lam_re, lam_im, log_step, b_re, b_im, c_re, c_im):
    gpb = SSM_GROUPS // SSM_BLOCKS
    lam = jnp.stack([lam_re, lam_im, jnp.broadcast_to(log_step[:, None], (SSM_GROUPS, SSM_P))])
    lam_x = jnp.repeat(lam, SSM_GROUP_CH, axis=1)
    b = jnp.stack([b_re, b_im]).transpose(0, 1, 3, 2).reshape(2, SSM_GROUPS * SSM_GROUP_CH, SSM_P)
    c = jnp.stack([c_re, c_im]).reshape(2, SSM_GROUPS * SSM_GROUP_CH, SSM_P)
    blk = lambda rows, width: pl.BlockSpec((rows, width), lambda k: (k, 0))
    stacked = lambda n, rows, width: pl.BlockSpec((n, rows, width), lambda k: (0, k, 0))
    mat = lambda rows, cols: pl.BlockSpec((1, rows, cols), lambda k: (k, 0, 0))
    return pl.pallas_call(
        _s5_prep_kernel,
        grid=(SSM_BLOCKS,),
        in_specs=[stacked(3, gpb, SSM_P), stacked(3, SSM_CH_BLOCK, SSM_P), stacked(2, SSM_CH_BLOCK, SSM_P),
                  stacked(2, SSM_CH_BLOCK, SSM_P)],
        out_specs=[blk(gpb, SSM_P), blk(gpb, SSM_P),
                   mat(SSM_CH_BLOCK, SSM_LANE_BLOCK), mat(SSM_CH_BLOCK, SSM_LANE_BLOCK),
                   mat(SSM_LANE_BLOCK, SSM_CH_BLOCK), mat(SSM_LANE_BLOCK, SSM_CH_BLOCK)],
        out_shape=[jax.ShapeDtypeStruct((SSM_GROUPS, SSM_P), F32)] * 2
        + [jax.ShapeDtypeStruct((SSM_BLOCKS, SSM_CH_BLOCK, SSM_LANE_BLOCK), BF16)] * 2
        + [jax.ShapeDtypeStruct((SSM_BLOCKS, SSM_LANE_BLOCK, SSM_CH_BLOCK), BF16)] * 2,
        compiler_params=_params(("parallel",)),
        name="s5_prep",
    )(lam, lam_x, b, c)


def _s5_dense_kernel(x_ref, gpre_ref, d_ref, lbr_ref, lbi_ref, bdr_ref, bdi_ref, cdr_ref, cdi_ref, s0r_ref, s0i_ref,
                     z_ref, sr_ref, si_ref, ur_ref, ui_ref, *, nb, tt):
    h = _rms(x_ref[...], gpre_ref[...])
    hb = h.astype(BF16)
    for kb in range(SSM_BLOCKS):
        ch = slice(kb * SSM_CH_BLOCK, (kb + 1) * SSM_CH_BLOCK)
        ln = slice(kb * SSM_LANE_BLOCK, (kb + 1) * SSM_LANE_BLOCK)
        ur_ref[...] = jnp.dot(hb[:, ch], bdr_ref[kb], preferred_element_type=F32)
        ui_ref[...] = jnp.dot(hb[:, ch], bdi_ref[kb], preferred_element_type=F32)
        a_re = jnp.broadcast_to(lbr_ref[:, ln], (nb, SSM_LANE_BLOCK))
        a_im = jnp.broadcast_to(lbi_ref[:, ln], (nb, SSM_LANE_BLOCK))
        s_re, s_im = s0r_ref[:, ln], s0i_ref[:, ln]
        for t in range(tt):
            rows = slice(t * nb, (t + 1) * nb)
            s_re, s_im = ((a_re * s_re - a_im * s_im) + ur_ref[rows, :],
                          (a_re * s_im + a_im * s_re) + ui_ref[rows, :])
            ur_ref[rows, :] = s_re
            ui_ref[rows, :] = s_im
        sr_ref[:, ln] = s_re
        si_ref[:, ln] = s_im
        y = (jnp.dot(ur_ref[...].astype(BF16), cdr_ref[kb], preferred_element_type=F32)
             - jnp.dot(ui_ref[...].astype(BF16), cdi_ref[kb], preferred_element_type=F32))
        y = y + d_ref[:, ch] * h[:, ch]
        z_ref[:, ch] = jax.nn.gelu(y).astype(z_ref.dtype)


def s5_dense(x, g_pre, d_skip, lbr, lbi, bdr, bdi, cdr, cdi, s0r, s0i, nb, tt):
    rows = nb * tt
    full = lambda shp: pl.BlockSpec(shp, lambda: tuple(0 for _ in shp))
    bd = full((SSM_BLOCKS, SSM_CH_BLOCK, SSM_LANE_BLOCK))
    cd = full((SSM_BLOCKS, SSM_LANE_BLOCK, SSM_CH_BLOCK))
    return pl.pallas_call(
        functools.partial(_s5_dense_kernel, nb=nb, tt=tt),
        in_specs=[full((rows, D_MODEL)), full((1, D_MODEL)), full((1, D_MODEL)), full((1, SSM_STATE)),
                  full((1, SSM_STATE)), bd, bd, cd, cd, full((nb, SSM_STATE)), full((nb, SSM_STATE))],
        out_specs=[full((rows, D_MODEL)), full((nb, SSM_STATE)), full((nb, SSM_STATE))],
        out_shape=[jax.ShapeDtypeStruct((rows, D_MODEL), BF16),
                   jax.ShapeDtypeStruct((nb, SSM_STATE), F32), jax.ShapeDtypeStruct((nb, SSM_STATE), F32)],
        scratch_shapes=[pltpu.VMEM((rows, SSM_LANE_BLOCK), F32), pltpu.VMEM((rows, SSM_LANE_BLOCK), F32)],
        compiler_params=pltpu.CompilerParams(vmem_limit_bytes=VMEM_LIMIT),
        name="s5_dense",
    )(x, g_pre.reshape(1, D_MODEL), d_skip.reshape(1, D_MODEL), lbr.reshape(1, SSM_STATE), lbi.reshape(1, SSM_STATE),
      bdr, bdi, cdr, cdi, s0r, s0i)


def _s5_seq_kernel(x_ref, gpre_ref, d_ref, lam_r_ref, lam_i_ref, bdr_ref, bdi_ref, cdr_ref, cdi_ref,
                   z_ref, sr_ref, si_ref, ur_ref, ui_ref, hs_ref, yo_ref):
    tt, pitch = S5_TT, S5_PITCH
    ti = pl.program_id(1)

    @pl.when(ti == 0)
    def _():
        sr_ref[...] = jnp.zeros(sr_ref.shape, F32)
        si_ref[...] = jnp.zeros(si_ref.shape, F32)

    h = _rms(x_ref[...], gpre_ref[...])
    zeros8 = jnp.zeros((SUBLANES, D_MODEL), F32)
    hs_ref[0:SUBLANES, :] = zeros8
    hs_ref[SUBLANES + tt:2 * SUBLANES + tt, :] = zeros8
    hs_ref[SUBLANES:SUBLANES + tt, :] = h
    hb = h.astype(BF16)
    hb_shift = hs_ref[4:tt + 12, :].astype(BF16)

    def even_rows(kb, lt):
        return pl.ds((kb * SUBLANES + 2 * lt) * pitch, tt)

    def odd_rows(kb, lt):
        return pl.ds((kb * SUBLANES + 2 * lt + 1) * pitch - 4, tt + SUBLANES)

    for kb in range(SSM_BLOCKS):
        ch = slice(kb * SSM_CH_BLOCK, (kb + 1) * SSM_CH_BLOCK)
        for u_ref, bd_ref in ((ur_ref, bdr_ref), (ui_ref, bdi_ref)):
            ue = jnp.dot(hb[:, ch], bd_ref[kb, :, 0:SSM_HALF], preferred_element_type=F32)
            uo = jnp.dot(hb_shift[:, ch], bd_ref[kb, :, SSM_HALF:SSM_LANE_BLOCK], preferred_element_type=F32)
            for lt in range(4):
                u_ref[even_rows(kb, lt), :] = ue[:, lt * LANES:(lt + 1) * LANES]
                u_ref[odd_rows(kb, lt), :] = uo[:, lt * LANES:(lt + 1) * LANES]

    blk = lambda q: slice(q * SUBLANES, (q + 1) * SUBLANES)
    a_re = [lam_r_ref[blk(q), :] for q in range(SSM_BLOCKS)]
    a_im = [lam_i_ref[blk(q), :] for q in range(SSM_BLOCKS)]

    def step(t, carry):
        new = []
        for q in range(SSM_BLOCKS):
            s_re, s_im = carry[2 * q], carry[2 * q + 1]
            rows = pl.ds(q * SUBLANES * pitch + t, SUBLANES, stride=pitch)
            n_re = (a_re[q] * s_re - a_im[q] * s_im) + ur_ref[rows, :]
            n_im = (a_re[q] * s_im + a_im[q] * s_re) + ui_ref[rows, :]
            ur_ref[rows, :] = n_re
            ui_ref[rows, :] = n_im
            new += [n_re, n_im]
        return tuple(new)

    init = []
    for q in range(SSM_BLOCKS):
        init += [sr_ref[0, blk(q), :], si_ref[0, blk(q), :]]
    final = lax.fori_loop(0, tt, step, tuple(init), unroll=2)
    for q in range(SSM_BLOCKS):
        sr_ref[0, blk(q), :] = final[2 * q]
        si_ref[0, blk(q), :] = final[2 * q + 1]

    for kb in range(SSM_BLOCKS):
        ch = slice(kb * SSM_CH_BLOCK, (kb + 1) * SSM_CH_BLOCK)

        def gather(u_ref, rows_of):
            return jnp.concatenate([u_ref[rows_of(kb, lt), :] for lt in range(4)], axis=1).astype(BF16)

        y_even = (jnp.dot(gather(ur_ref, even_rows), cdr_ref[kb, 0:SSM_HALF, :], preferred_element_type=F32)
                  - jnp.dot(gather(ui_ref, even_rows), cdi_ref[kb, 0:SSM_HALF, :], preferred_element_type=F32))
        yo_ref[...] = (jnp.dot(gather(ur_ref, odd_rows), cdr_ref[kb, SSM_HALF:SSM_LANE_BLOCK, :],
                               preferred_element_type=F32)
                       - jnp.dot(gather(ui_ref, odd_rows), cdi_ref[kb, SSM_HALF:SSM_LANE_BLOCK, :],
                                 preferred_element_type=F32))
        y = y_even + yo_ref[4:tt + 4, :]
        y = y + d_ref[:, ch] * h[:, ch]
        z_ref[:, ch] = jax.nn.gelu(y).astype(z_ref.dtype)


def s5_seq(x, g_pre, d_skip, lam_r, lam_i, bdr, bdi, cdr, cdi, nseq):
    m = x.shape[0]
    ntt = m // (nseq * S5_TT)
    fixed2 = lambda shp: pl.BlockSpec(shp, lambda si, ti: (0, 0))
    state = pl.BlockSpec((1, SSM_SLABS, LANES), lambda si, ti: (si, 0, 0))
    bd = _resident((SSM_BLOCKS, SSM_CH_BLOCK, SSM_LANE_BLOCK), lambda si, ti: (0, 0, 0))
    cd = _resident((SSM_BLOCKS, SSM_LANE_BLOCK, SSM_CH_BLOCK), lambda si, ti: (0, 0, 0))
    return pl.pallas_call(
        _s5_seq_kernel,
        grid=(nseq, ntt),
        in_specs=[pl.BlockSpec((S5_TT, D_MODEL), lambda si, ti: (si * ntt + ti, 0)),
                  fixed2((1, D_MODEL)), fixed2((1, D_MODEL)), fixed2((SSM_SLABS, LANES)), fixed2((SSM_SLABS, LANES)),
                  bd, bd, cd, cd],
        out_specs=[pl.BlockSpec((S5_TT, D_MODEL), lambda si, ti: (si * ntt + ti, 0)), state, state],
        out_shape=[jax.ShapeDtypeStruct((m, D_MODEL), BF16),
                   jax.ShapeDtypeStruct((nseq, SSM_SLABS, LANES), F32),
                   jax.ShapeDtypeStruct((nseq, SSM_SLABS, LANES), F32)],
        scratch_shapes=[pltpu.VMEM((SSM_SLABS * S5_PITCH, LANES), F32), pltpu.VMEM((SSM_SLABS * S5_PITCH, LANES), F32),
                        pltpu.VMEM((S5_TT + 2 * SUBLANES, D_MODEL), F32),
                        pltpu.VMEM((S5_TT + SUBLANES, SSM_CH_BLOCK), F32)],
        compiler_params=_params(("parallel", "arbitrary")),
        name="s5_seq",
    )(x, g_pre.reshape(1, D_MODEL), d_skip.reshape(1, D_MODEL), lam_r, lam_i, bdr, bdi, cdr, cdi)


def _to_slabs(v):
    lead = v.shape[:-1]
    v = v.reshape(lead + (SSM_BLOCKS, 2, SUBLANES // 2, LANES))
    return v.swapaxes(-3, -2).reshape(lead + (SSM_SLABS, LANES))


def _from_slabs(s):
    lead = s.shape[:-2]
    s = s.reshape(lead + (SSM_BLOCKS, SUBLANES // 2, 2, LANES))
    return s.swapaxes(-3, -2).reshape(lead + (SSM_STATE,))


def _slopes():
    return 2.0 ** (-8.0 * jnp.arange(1, SWA_Q_HEADS + 1, dtype=F32) / SWA_Q_HEADS)


def kernel(x_prompt, x_sample, state_ret, cache_swa_k, cache_swa_v, state_ssm_re, state_ssm_im, state_ffn_conv, norm_mix_pre, norm_mix_post, norm_ffn_pre, norm_ffn_post, w_in_even, w_out_even, swa_sinks, ssm_lam_re, ssm_lam_im, ssm_log_step, ssm_b_re, ssm_b_im, ssm_c_re, ssm_c_im, ssm_d, w_glu, ffn_w_a, ffn_w_g, ffn_conv_w, ffn_conv_b, ffn_w_down):
    pb, pl_len, _ = x_prompt.shape
    sb, sl_len, _ = x_sample.shape
    mp = pb * pl_len
    ms = sb * sl_len
    s_pad = BF16_ROWS
    carry_rows_s = (CONV_W - 1) * sb
    slopes = _slopes()

    xp = x_prompt.reshape(mp, D_MODEL)
    xs = x_sample.transpose(1, 0, 2).reshape(ms, D_MODEL)

    hp = rmsnorm_bf16(xp, norm_mix_pre[0], 512)
    hs = rmsnorm_bf16(xs, norm_mix_pre[0], ms)

    assert state_ret.shape[0] >= 2
    zero_ret = jnp.zeros((1, pb, RET_HEADS, RET_DK, RET_DV), F32)
    ret_p, ret_s, wk_p, wk_s, wv_p, wv_s = [], [], [], [], [], []
    sre_p, sre_s, sim_p, sim_s, conv_p, conv_s = [], [], [], [], [], []
    w_glu_b = None

    for layer in range(DEPTH):
        i = layer // 2
        if layer % 2 == 0:
            proj, proj_s, w_out_b = in_proj(hp, hs, w_in_even, w_out_even, i, 1024, 768)
            proj = proj.reshape(pb, pl_len, EVEN_IN)
            last_even = i == state_ret.shape[0] - 1
            o_ret, s_ret = retention(proj, zero_ret, 0, ret_p if last_even else [], RET_CHUNK, RET_CHUNK, 1,
                                     RET_CHUNKS_PER_STEP)
            ret_p = s_ret if last_even else ret_p + [s_ret]
            kcol, vcol = KV_COL // LANES, KV_COL // LANES + 1
            blk = lambda c, prev: pl.BlockSpec(
                (1, SWA_BLOCK, LANES),
                (lambda bi, qi: (bi, jnp.maximum(qi - 1, 0), c)) if prev else (lambda bi, qi: (bi, qi, c)))
            o_swa = swa_attention(
                slopes, swa_sinks[i], proj,
                pl.BlockSpec((1, SWA_BLOCK, SWA_WIDTH), lambda bi, qi: (bi, qi, Q_COL // SWA_WIDTH)),
                [proj, proj, proj, proj], [blk(kcol, True), blk(vcol, True), blk(kcol, False), blk(vcol, False)],
                pb, 1, 1, pl_len // SWA_BLOCK, SWA_BLOCK, 0, SWA_BLOCK)
            kv_shape = (pb, WINDOW, SWA_KV_HEADS, SWA_HD)
            wk_p.append(proj[:, pl_len - WINDOW:, KV_COL:KV_COL + KV_WIDTH].reshape(kv_shape))
            wv_p.append(proj[:, pl_len - WINDOW:, KV_COL + KV_WIDTH:KV_COL + 2 * KV_WIDTH].reshape(kv_shape))

            proj_bt = proj_s.reshape(sl_len, sb, EVEN_IN).transpose(1, 0, 2)
            proj_pad = jnp.pad(proj_bt, ((0, 0), (0, s_pad - sl_len), (0, 0)))
            o_ret_s, s_ret_s = retention(proj_pad, state_ret, i, ret_s if last_even else [], sl_len, s_pad,
                                         SAMPLE_SEQS_PER_STEP, 1)
            ret_s = s_ret_s if last_even else ret_s + [s_ret_s]
            win = cache_swa_k.shape[2]
            n_even = cache_swa_k.shape[0]
            ck = cache_swa_k.reshape(n_even, sb, win, KV_WIDTH)
            cv = cache_swa_v.reshape(n_even, sb, win, KV_WIDTH)
            cache_spec = pl.BlockSpec((None, SAMPLE_SEQS_PER_STEP, win, KV_WIDTH), lambda bi, qi: (i, bi, 0, 0))
            cur = lambda c: pl.BlockSpec((SAMPLE_SEQS_PER_STEP, s_pad, LANES), lambda bi, qi: (bi, 0, c))
            o_swa_s = swa_attention(
                slopes, swa_sinks[i], proj_pad,
                pl.BlockSpec((SAMPLE_SEQS_PER_STEP, s_pad, SWA_WIDTH), lambda bi, qi: (bi, 0, Q_COL // SWA_WIDTH)),
                [ck, cv, proj_pad, proj_pad], [cache_spec, cache_spec, cur(kcol), cur(vcol)],
                sb, SAMPLE_SEQS_PER_STEP, SWA_CHAINS, 1, s_pad, win, 0)
            k_new = proj_bt[:, :, KV_COL:KV_COL + KV_WIDTH]
            v_new = proj_bt[:, :, KV_COL + KV_WIDTH:KV_COL + 2 * KV_WIDTH]
            kv_shape_s = (sb, win, SWA_KV_HEADS, SWA_HD)
            wk_s.append(jnp.concatenate([ck[i, :, sl_len:], k_new], axis=1).reshape(kv_shape_s))
            wv_s.append(jnp.concatenate([cv[i, :, sl_len:], v_new], axis=1).reshape(kv_shape_s))
            tb = lambda o: o[:, :sl_len].transpose(1, 0, 2).reshape(ms, -1)

            xp, xs, hp, hs, _ = proj_residual(
                [o_ret.reshape(mp, RET_WIDTH), o_swa.reshape(mp, SWA_WIDTH)], [tb(o_ret_s), tb(o_swa_s)],
                w_out_b, [0, 1], False, xp, xs, norm_mix_post[layer], norm_ffn_pre[layer], 512, RES_ROW_PARTS,
                "out_proj")
        else:
            lbr, lbi, bdr, bdi, cdr, cdi = s5_prep(ssm_lam_re[i], ssm_lam_im[i], ssm_log_step[i], ssm_b_re[i],
                                                   ssm_b_im[i], ssm_c_re[i], ssm_c_im[i])
            lam_r = _to_slabs(lbr.reshape(SSM_STATE))
            lam_i = _to_slabs(lbi.reshape(SSM_STATE))
            zp, s_re, s_im = s5_seq(xp, norm_mix_pre[layer], ssm_d[i], lam_r, lam_i, bdr, bdi, cdr, cdi, pb)
            sre_p.append(_from_slabs(s_re).reshape(pb, SSM_GROUPS, SSM_P))
            sim_p.append(_from_slabs(s_im).reshape(pb, SSM_GROUPS, SSM_P))
            zs, s_re_s, s_im_s = s5_dense(xs, norm_mix_pre[layer], ssm_d[i], lbr, lbi, bdr, bdi, cdr, cdi,
                                          state_ssm_re[i].reshape(sb, SSM_STATE),
                                          state_ssm_im[i].reshape(sb, SSM_STATE), sb, sl_len)
            sre_s.append(s_re_s.reshape(sb, SSM_GROUPS, SSM_P))
            sim_s.append(s_im_s.reshape(sb, SSM_GROUPS, SSM_P))
            xp, xs, hp, hs, _ = proj_residual([zp], [zs], w_glu_b, [0], True, xp, xs, norm_mix_post[layer],
                                              norm_ffn_pre[layer], 256, RES_ROW_PARTS, "glu_proj")

        init_s = state_ffn_conv[layer].transpose(1, 0, 2).reshape(carry_rows_s, D_FF)
        act, act_s, tail, tail_s, w_down_b = ffn_up(hp, hs, ffn_w_a, ffn_w_g, ffn_w_down, layer, ffn_conv_w,
                                                    ffn_conv_b, init_s, sb, 1024, FFN_TF, pl_len)
        conv_p.append(tail[:, SUBLANES - (CONV_W - 1):])
        conv_s.append(tail_s.reshape(CONV_W - 1, sb, D_FF).transpose(1, 0, 2))
        next_even = layer + 1 < DEPTH and (layer + 1) % 2 == 0
        next_odd = layer + 1 < DEPTH and (layer + 1) % 2 == 1
        xp, xs, hp, hs, w_cast = proj_residual(
            [act], [act_s], w_down_b, [0], False, xp, xs, norm_ffn_post[layer],
            norm_mix_pre[layer + 1] if next_even else None, 256, 1, "ffn_down",
            cast=(w_glu, (layer + 1) // 2) if next_odd else None)
        if next_odd:
            w_glu_b = w_cast

    y_prompt = xp.reshape(pb, pl_len, D_MODEL)
    y_sample = xs.reshape(sl_len, sb, D_MODEL).transpose(1, 0, 2)
    return (y_prompt, y_sample, ret_p, ret_s, jnp.stack(wk_p), jnp.stack(wk_s),
            jnp.stack(wv_p), jnp.stack(wv_s), jnp.stack(sre_p), jnp.stack(sre_s), jnp.stack(sim_p),
            jnp.stack(sim_s), jnp.stack(conv_p), jnp.stack(conv_s))
```

```python
import functools

import jax
import jax.numpy as jnp
from jax import lax
from jax.experimental import pallas as pl
from jax.experimental.pallas import tpu as pltpu

F32 = jnp.float32
BF16 = jnp.bfloat16

D_MODEL = 2048
DEPTH = 4
RET_HEADS = 8
RET_DK = 128
RET_DV = 128
RET_CHUNK = 128
RET_WIDTH = RET_HEADS * RET_DV
SWA_Q_HEADS = 16
SWA_KV_HEADS = 2
SWA_HD = 64
WINDOW = 128
SWA_BLOCK = 128
SWA_WIDTH = SWA_Q_HEADS * SWA_HD
SWA_CHAINS = 4
RET_CHUNKS_PER_STEP = 4
SAMPLE_SEQS_PER_STEP = 4

NORM_TM = 512
IN_PROJ_TM, IN_PROJ_TN = 1024, 768
W_OUT_CAST_ROWS = 512
OUT_PROJ_TM, OUT_PROJ_ROW_PARTS = 512, 2
GLU_TM, GLU_ROW_PARTS = 256, 2
FFN_UP_TM, FFN_TF = 1024, 512
FFN_DOWN_TM, FFN_DOWN_ROW_PARTS = 256, 1
EVEN_IN = 5376
Q_COL = RET_WIDTH * 4
KV_COL = Q_COL + SWA_WIDTH
KV_WIDTH = SWA_KV_HEADS * SWA_HD
SSM_GROUP_CH = 16
SSM_GROUPS = 128
SSM_P = 64
SSM_STATE = SSM_GROUPS * SSM_P
SSM_LANE_BLOCK = 1024
SSM_CH_BLOCK = 256
SSM_BLOCKS = SSM_STATE // SSM_LANE_BLOCK
SSM_HALF = SSM_LANE_BLOCK // 2
D_FF = 5632
CONV_W = 3
NORM_EPS = 1e-6

LANES = 128
SUBLANES = 8
BF16_ROWS = 16
VMEM_LIMIT = 56 * 1024 * 1024

SSM_SLABS = SSM_STATE // LANES
S5_TT = 256
S5_PITCH = S5_TT + 4


def _params(sem, vmem=VMEM_LIMIT):
    return pltpu.CompilerParams(dimension_semantics=sem, vmem_limit_bytes=vmem)


def _rms(x, g):
    return x * lax.rsqrt(jnp.mean(x * x, axis=-1, keepdims=True) + NORM_EPS) * g


def _resident(shape, index_map):
    return pl.BlockSpec(shape, index_map, pipeline_mode=pl.Buffered(1))


def _norm_kernel(x_ref, g_ref, o_ref):
    o_ref[...] = _rms(x_ref[...], g_ref[...]).astype(o_ref.dtype)


def rmsnorm_bf16(x, g, tm):
    m, d = x.shape
    return pl.pallas_call(
        _norm_kernel,
        grid=(m // tm,),
        in_specs=[pl.BlockSpec((tm, d), lambda i: (i, 0)), pl.BlockSpec((1, d), lambda i: (0, 0))],
        out_specs=pl.BlockSpec((tm, d), lambda i: (i, 0)),
        out_shape=jax.ShapeDtypeStruct((m, d), BF16),
        compiler_params=_params(("parallel",)),
        name="rmsnorm",
    )(x, g.reshape(1, d))


def _in_proj_kernel(x_ref, xs_ref, w_ref, wo_ref, o_ref, os_ref, wob_ref, wb_ref):
    @pl.when(pl.program_id(1) == 0)
    def _():
        wb_ref[...] = w_ref[...].astype(BF16)
        wob_ref[...] = wo_ref[...].astype(BF16)
        os_ref[...] = jnp.dot(xs_ref[...], wb_ref[...], preferred_element_type=F32)

    o_ref[...] = jnp.dot(x_ref[...], wb_ref[...], preferred_element_type=F32)


def in_proj(h, hs, w_in, w_out, li, tm, tn):
    m, ms = h.shape[0], hs.shape[0]
    n_tiles = EVEN_IN // tn
    slab = W_OUT_CAST_ROWS
    n_slabs = D_MODEL // slab
    assert n_slabs <= n_tiles
    slab_idx = lambda j, i: (li, jnp.minimum(j, n_slabs - 1), 0)
    return pl.pallas_call(
        _in_proj_kernel,
        grid=(n_tiles, m // tm),
        in_specs=[pl.BlockSpec((tm, D_MODEL), lambda j, i: (i, 0)),
                  pl.BlockSpec((ms, D_MODEL), lambda j, i: (0, 0)),
                  pl.BlockSpec((None, D_MODEL, tn), lambda j, i: (li, 0, j)),
                  pl.BlockSpec((None, slab, D_MODEL), slab_idx)],
        out_specs=[pl.BlockSpec((tm, tn), lambda j, i: (i, j)),
                   pl.BlockSpec((ms, tn), lambda j, i: (0, j)),
                   pl.BlockSpec((slab, D_MODEL), lambda j, i: (jnp.minimum(j, n_slabs - 1), 0))],
        out_shape=[jax.ShapeDtypeStruct((m, EVEN_IN), F32), jax.ShapeDtypeStruct((ms, EVEN_IN), F32),
                   jax.ShapeDtypeStruct((D_MODEL, D_MODEL), BF16)],
        scratch_shapes=[pltpu.VMEM((D_MODEL, tn), BF16)],
        compiler_params=_params(("arbitrary", "arbitrary")),
        name="in_proj",
    )(h, hs, w_in, w_out)


def _retention_kernel(decay_ref, q_ref, k_ref, v_ref, g_ref, intra_ref, read_ref, write_ref, s0_ref, *rest,
                      nseq, chunks, c):
    prev_refs, o_ref, out_ref = rest[:-2], rest[-2], rest[-1]
    s_ref = out_ref.at[len(prev_refs)] if prev_refs else out_ref
    ci = pl.program_id(1)

    @pl.when(ci == 0)
    def _():
        s_ref[...] = s0_ref[...]
        for j, prev_ref in enumerate(prev_refs):
            out_ref[j] = prev_ref[...]

    heads = [slice(h * RET_DK, (h + 1) * RET_DK) for h in range(RET_HEADS)]
    nt = (((1,), (1,)), ((), ()))
    tn = (((0,), (0,)), ((), ()))
    jobs = [(b, h) for b in range(nseq) for h in range(RET_HEADS)]
    for cc in range(chunks):
        rows = slice(cc * c, (cc + 1) * c)
        first = []
        for b, h in jobs:
            sl = heads[h]
            q = q_ref[b, rows, sl].astype(BF16)
            k = k_ref[b, rows, sl] * (RET_DK ** -0.5)
            v = v_ref[b, rows, sl].astype(BF16)
            s = s_ref[b, h]
            sc = lax.dot_general(q, k.astype(BF16), nt, preferred_element_type=F32)
            qs = jnp.dot(q, s.astype(BF16), preferred_element_type=F32)
            kv = lax.dot_general((k * write_ref[h]).astype(BF16), v, tn, preferred_element_type=F32)
            first.append((v, s, sc, qs, kv))
        outs = []
        for (b, h), (v, s, sc, qs, kv) in zip(jobs, first):
            outs.append(jnp.dot((sc * intra_ref[h]).astype(BF16), v, preferred_element_type=F32) + qs * read_ref[h])
            s_ref[b, h] = s * decay_ref[h] + kv
        for (b, h), o in zip(jobs, outs):
            g = g_ref[b, rows, heads[h]]
            o = o * lax.rsqrt(jnp.mean(o * o, axis=-1, keepdims=True) + NORM_EPS)
            o_ref[b, rows, heads[h]] = (o * (g * jax.nn.sigmoid(g))).astype(o_ref.dtype)


def _retention_tables(c_real, c_pad):
    lg = jnp.log(1.0 - 2.0 ** (-5.0 - jnp.arange(RET_HEADS, dtype=F32)))
    idx = jnp.arange(c_pad, dtype=F32)
    diff = idx[:, None] - idx[None, :]
    intra = jnp.where(diff >= 0, jnp.exp(lg[:, None, None] * jnp.maximum(diff, 0.0)), 0.0)
    read = jnp.exp(lg[:, None] * (idx[None, :] + 1.0))
    write = jnp.exp(lg[:, None] * (c_real - 1.0 - idx[None, :]))
    decay = jnp.exp(lg * c_real)
    bshape = (RET_HEADS, c_pad, RET_DV)
    return decay, intra, jnp.broadcast_to(read[:, :, None], bshape), jnp.broadcast_to(write[:, :, None], bshape)


def retention(proj, s0, li, prev_states, c_real, c_pad, nseq, chunks):
    b, l, _ = proj.shape
    rows = chunks * c_pad
    decay, intra, read, write = _retention_tables(c_real, c_pad)

    def col(j):
        return pl.BlockSpec((nseq, rows, RET_WIDTH), lambda bi, ci: (bi, ci, j))

    state_shape = (nseq, RET_HEADS, RET_DK, RET_DV)
    state_spec = pl.BlockSpec(state_shape, lambda bi, ci: (bi, 0, 0, 0))
    out_state_shape = s0.shape[1:]
    out_state_spec = state_spec
    if prev_states:
        n_out = len(prev_states) + 1
        out_state_shape = (n_out,) + out_state_shape
        out_state_spec = pl.BlockSpec((n_out,) + state_shape, lambda bi, ci: (0, bi, 0, 0, 0))
    return pl.pallas_call(
        functools.partial(_retention_kernel, nseq=nseq, chunks=chunks, c=c_pad),
        grid=(b // nseq, l // rows),
        in_specs=[
            pl.BlockSpec(memory_space=pltpu.SMEM),
            col(0), col(1), col(2), col(3),
            pl.BlockSpec((RET_HEADS, c_pad, c_pad), lambda bi, ci: (0, 0, 0)),
            pl.BlockSpec((RET_HEADS, c_pad, RET_DV), lambda bi, ci: (0, 0, 0)),
            pl.BlockSpec((RET_HEADS, c_pad, RET_DV), lambda bi, ci: (0, 0, 0)),
            pl.BlockSpec((None,) + state_shape, lambda bi, ci: (li, bi, 0, 0, 0)),
        ] + [state_spec] * len(prev_states),
        out_specs=[pl.BlockSpec((nseq, rows, RET_WIDTH), lambda bi, ci: (bi, ci, 0)), out_state_spec],
        out_shape=[jax.ShapeDtypeStruct((b, l, RET_WIDTH), BF16), jax.ShapeDtypeStruct(out_state_shape, F32)],
        compiler_params=_params(("parallel", "arbitrary")),
        name="retention",
    )(decay, proj, proj, proj, proj, intra, read, write, s0, *prev_states)


def _swa_kernel(bias_p_ref, bias_c_ref, sink_ref, q_ref, kp_ref, vp_ref, kc_ref, vc_ref, o_ref, *,
                nseq, group, tq, q0_base, q0_step):
    prev_always_visible = q0_step == 0 and q0_base >= WINDOW
    if not prev_always_visible:
        q0 = q0_base + pl.program_id(1) * q0_step
        prev_penalty = jnp.where(q0 >= WINDOW, 0.0, -jnp.inf).astype(F32)
    lane_lo = lax.broadcasted_iota(jnp.int32, (tq, LANES), 1) < SWA_HD
    scale = SWA_HD ** -0.5
    nt = (((1,), (1,)), ((), ()))
    pairs_per_kv = SWA_Q_HEADS // SWA_KV_HEADS // 2
    lane_sum = lambda x: jnp.sum(x, axis=-1, keepdims=True)
    lane_max = lambda x: jnp.max(x, axis=-1, keepdims=True)

    def both(ref, b):
        x = ref[b]
        return x.astype(BF16), pltpu.roll(x, SWA_HD, 1).astype(BF16)

    keys = [(both(kp_ref, b), both(kc_ref, b)) for b in range(nseq)]
    vals = [(both(vp_ref, b), both(vc_ref, b)) for b in range(nseq)]
    all_chains = [(b, kv, parity) for b in range(nseq) for kv in range(SWA_KV_HEADS) for parity in range(2)]
    outs = {}
    for g0 in range(0, len(all_chains), group):
        chains = all_chains[g0:g0 + group]
        scores = []
        for b, kv, parity in chains:
            swapped = (kv == 0) != (parity == 0)
            (kp, kp_sw), (kc, kc_sw) = keys[b]
            kprev, kcur = (kp_sw, kc_sw) if swapped else (kp, kc)
            qs = []
            for p in range(pairs_per_kv):
                pair = kv * pairs_per_kv + p
                qp = q_ref[b, :, pair * LANES:(pair + 1) * LANES]
                qs.append(jnp.where(lane_lo, qp, 0.0) if parity == 0 else jnp.where(lane_lo, 0.0, qp))
            qh = jnp.concatenate(qs, axis=0).astype(BF16)
            scores.append((lax.dot_general(qh, kprev, nt, preferred_element_type=F32),
                           lax.dot_general(qh, kcur, nt, preferred_element_type=F32)))
        probs = []
        for (b, kv, parity), (sp, sc) in zip(chains, scores):
            chain = 2 * kv + parity
            sink = sink_ref[chain][:, :1]
            sp = sp * scale + bias_p_ref[chain]
            sc = sc * scale + bias_c_ref[chain]
            if not prev_always_visible:
                sp = sp + prev_penalty
            if tq == WINDOW:
                m = jnp.maximum(lane_max(jnp.maximum(sp, sc)), sink)
            else:
                m = jnp.maximum(jnp.maximum(lane_max(sp), lane_max(sc)), sink)
            pp = jnp.exp(sp - m)
            pc = jnp.exp(sc - m)
            psum = lane_sum(pp + pc) if tq == WINDOW else lane_sum(pp) + lane_sum(pc)
            denom = psum + jnp.exp(sink - m)
            probs.append(((pp / denom).astype(BF16), (pc / denom).astype(BF16)))
        for (b, kv, parity), (pp, pc) in zip(chains, probs):
            swapped = (kv == 0) != (parity == 0)
            (vp, vp_sw), (vc, vc_sw) = vals[b]
            vprev, vcur = (vp_sw, vc_sw) if swapped else (vp, vc)
            outs[b, kv, parity] = (jnp.dot(pp, vprev, preferred_element_type=F32)
                                   + jnp.dot(pc, vcur, preferred_element_type=F32))
        for b, kv, parity in chains:
            if parity == 0:
                continue
            for p in range(pairs_per_kv):
                pair = kv * pairs_per_kv + p
                rs = slice(p * tq, (p + 1) * tq)
                o_ref[b, :, pair * LANES:(pair + 1) * LANES] = jnp.where(lane_lo, outs[b, kv, 0][rs],
                                                                          outs[b, kv, 1][rs]).astype(o_ref.dtype)


def _swa_tables(slopes, sinks, tq):
    per = SWA_Q_HEADS // SWA_CHAINS

    def rows_of(vals):
        v = vals.astype(F32).reshape(SWA_KV_HEADS, per, 2).transpose(0, 2, 1).reshape(SWA_CHAINS, per)
        return jnp.broadcast_to(v[:, :, None, None], (SWA_CHAINS, per, tq, 1))

    r = jnp.arange(tq, dtype=jnp.int32)[:, None]
    dist_p = r - jnp.arange(WINDOW, dtype=jnp.int32)[None, :] + WINDOW
    dist_c = r - jnp.arange(tq, dtype=jnp.int32)[None, :]
    slope = rows_of(slopes)
    bias_p = jnp.where(dist_p <= WINDOW, -(slope * dist_p.astype(F32)), -jnp.inf)
    bias_c = jnp.where(dist_c >= 0, -(slope * dist_c.astype(F32)), -jnp.inf)
    sink = jnp.broadcast_to(rows_of(sinks), (SWA_CHAINS, per, tq, LANES))
    flat = lambda t: t.reshape(SWA_CHAINS, per * tq, t.shape[-1])
    return flat(bias_p), flat(bias_c), flat(sink)


def swa_attention(slopes, sinks, q_arr, q_spec, kv_arrs, kv_specs, b, nseq, group, nb, tq, q0_base, q0_step):
    assert q0_base % WINDOW == 0 and q0_step % WINDOW == 0
    rows = (SWA_Q_HEADS // SWA_CHAINS) * tq
    table = lambda width: pl.BlockSpec((SWA_CHAINS, rows, width), lambda bi, i: (0, 0, 0))
    return pl.pallas_call(
        functools.partial(_swa_kernel, nseq=nseq, group=group, tq=tq, q0_base=q0_base, q0_step=q0_step),
        grid=(b // nseq, nb),
        in_specs=[table(WINDOW), table(tq), table(LANES), q_spec] + kv_specs,
        out_specs=pl.BlockSpec((nseq, tq, SWA_WIDTH), lambda bi, i: (bi, i, 0)),
        out_shape=jax.ShapeDtypeStruct((b, nb * tq, SWA_WIDTH), BF16),
        compiler_params=_params(("parallel", "parallel")),
        name="swa",
    )(*_swa_tables(slopes, sinks, tq), q_arr, *kv_arrs)


def _res_tile(acts, ws, glu, x_ref, gpost_ref, gnext_ref, xo_ref, ho_ref, parts=1):
    rows = x_ref.shape[0] // parts
    ys = []
    for p in range(parts):
        rs = slice(p * rows, (p + 1) * rows)
        y = jnp.dot(acts[0][rs, :], ws[0][...], preferred_element_type=F32)
        for a, w in zip(acts[1:], ws[1:]):
            y = y + jnp.dot(a[rs, :], w[...], preferred_element_type=F32)
        ys.append(y)
    for p, y in enumerate(ys):
        rs = slice(p * rows, (p + 1) * rows)
        if glu:
            y = y[:, :D_MODEL] * jax.nn.sigmoid(y[:, D_MODEL:])
        x_new = x_ref[rs, :] + _rms(y, gpost_ref[...])
        xo_ref[rs, :] = x_new
        if ho_ref is not None:
            ho_ref[rs, :] = _rms(x_new, gnext_ref[...]).astype(ho_ref.dtype)


def _proj_res_kernel(*refs, n_pieces, glu, has_next, has_cast, parts):
    it = iter(refs)
    take = lambda k: [next(it) for _ in range(k)]
    acts, acts_s, ws = take(n_pieces), take(n_pieces), take(n_pieces)
    x_ref, xs_ref, gpost_ref = take(3)
    gnext_ref = next(it) if has_next else None
    src_ref = next(it) if has_cast else None
    xo_ref, xos_ref = take(2)
    ho_ref, hos_ref = take(2) if has_next else (None, None)
    dst_ref = next(it) if has_cast else None

    _res_tile(acts, ws, glu, x_ref, gpost_ref, gnext_ref, xo_ref, ho_ref, parts=parts)

    @pl.when(pl.program_id(0) == 0)
    def _():
        _res_tile(acts_s, ws, glu, xs_ref, gpost_ref, gnext_ref, xos_ref, hos_ref)

    if has_cast:
        dst_ref[...] = src_ref[...].astype(dst_ref.dtype)


def proj_residual(acts, acts_s, w, w_row_blocks, glu, x, xs, g_post, g_next, tm, parts, name, cast=None):
    m, ms = x.shape[0], xs.shape[0]
    steps = m // tm
    n_pieces = len(acts)
    has_next = g_next is not None
    row = lambda i: (i, 0)
    fixed = lambda i: (0, 0)
    in_specs = [pl.BlockSpec((tm, a.shape[1]), row) for a in acts]
    in_specs += [pl.BlockSpec((ms, a.shape[1]), fixed) for a in acts_s]
    args = list(acts) + list(acts_s)
    for p in range(n_pieces):
        in_specs.append(_resident((acts[p].shape[1], w.shape[1]),
                                  functools.partial(lambda i, rb: (rb, 0), rb=w_row_blocks[p])))
        args.append(w)
    in_specs += [pl.BlockSpec((tm, D_MODEL), row), pl.BlockSpec((ms, D_MODEL), fixed), pl.BlockSpec((1, D_MODEL), fixed)]
    args += [x, xs, g_post.reshape(1, D_MODEL)]
    if has_next:
        in_specs.append(pl.BlockSpec((1, D_MODEL), fixed))
        args.append(g_next.reshape(1, D_MODEL))
    if cast is not None:
        src, li = cast
        slab = src.shape[1] // steps
        in_specs.append(pl.BlockSpec((None, slab, src.shape[2]), lambda i: (li, i, 0)))
        args.append(src)
    out_specs = [pl.BlockSpec((tm, D_MODEL), row), pl.BlockSpec((ms, D_MODEL), fixed)]
    out_shape = [jax.ShapeDtypeStruct((m, D_MODEL), F32), jax.ShapeDtypeStruct((ms, D_MODEL), F32)]
    if has_next:
        out_specs += [pl.BlockSpec((tm, D_MODEL), row), pl.BlockSpec((ms, D_MODEL), fixed)]
        out_shape += [jax.ShapeDtypeStruct((m, D_MODEL), BF16), jax.ShapeDtypeStruct((ms, D_MODEL), BF16)]
    if cast is not None:
        out_specs.append(pl.BlockSpec((slab, src.shape[2]), row))
        out_shape.append(jax.ShapeDtypeStruct(src.shape[1:], BF16))
    outs = pl.pallas_call(
        functools.partial(_proj_res_kernel, n_pieces=n_pieces, glu=glu, has_next=has_next, has_cast=cast is not None,
                          parts=parts),
        grid=(steps,),
        in_specs=in_specs,
        out_specs=out_specs,
        out_shape=out_shape,
        compiler_params=_params(("arbitrary",)),
        name=name,
    )(*args)
    outs = list(outs)
    x_new, xs_new = outs[0], outs[1]
    h_new, hs_new = (outs[2], outs[3]) if has_next else (None, None)
    w_cast = outs[-1] if cast is not None else None
    return x_new, xs_new, h_new, hs_new, w_cast


def _ffn_rows(x_ref, w_ref, cw_ref, cb_ref, carry_ref, act_ref, *, rs):
    tf = act_ref.shape[1]
    ag = jnp.dot(x_ref[...], w_ref[...], preferred_element_type=F32)
    a, g = ag[:, :tf], ag[:, tf:]
    cr = carry_ref.shape[0]
    ext = jnp.concatenate([carry_ref[...], a], axis=0)
    conv = cb_ref[...] + cw_ref[0:1, :] * pltpu.roll(ext, 2 * rs, 0)[cr:]
    conv = conv + cw_ref[1:2, :] * pltpu.roll(ext, rs, 0)[cr:]
    conv = conv + cw_ref[2:3, :] * a
    act_ref[...] = (jax.nn.gelu(conv) * g).astype(act_ref.dtype)
    carry_ref[...] = ext[a.shape[0]:]


def _ffn_up_kernel(x_ref, xs_ref, wa_ref, wg_ref, wd_ref, cw_ref, cb_ref, init_s_ref,
                   act_ref, act_s_ref, tail_ref, tail_s_ref, wdb_ref, w_ref, carry_ref, carry_s_ref, *,
                   tiles_per_seq, rs_s):
    i = pl.program_id(1)
    t_in_seq = i % tiles_per_seq
    tile = functools.partial(_ffn_rows, w_ref=w_ref, cw_ref=cw_ref, cb_ref=cb_ref)
    tf = act_ref.shape[1]

    @pl.when(i == 0)
    def _():
        w_ref[:, :tf] = wa_ref[...].astype(BF16)
        w_ref[:, tf:] = wg_ref[...].astype(BF16)
        wdb_ref[...] = wd_ref[...].astype(BF16)
        carry_s_ref[...] = init_s_ref[...]
        tile(xs_ref, carry_ref=carry_s_ref, act_ref=act_s_ref, rs=rs_s)
        tail_s_ref[...] = carry_s_ref[...]

    @pl.when(t_in_seq == 0)
    def _():
        carry_ref[...] = jnp.zeros(carry_ref.shape, F32)

    tile(x_ref, carry_ref=carry_ref, act_ref=act_ref, rs=1)

    @pl.when(t_in_seq == tiles_per_seq - 1)
    def _():
        tail_ref[0] = carry_ref[...]


def ffn_up(h, hs, wa, wg, wd, li, conv_w, conv_b, init_s, rs_s, tm, tf, rows_per_seq):
    m, ms = h.shape[0], hs.shape[0]
    cr = SUBLANES
    cr_s = init_s.shape[0]
    tiles_per_seq = rows_per_seq // tm
    nseq = m // rows_per_seq
    n_tiles = D_FF // tf
    slab = D_FF // n_tiles
    col = lambda f, i: (0, f)
    return pl.pallas_call(
        functools.partial(_ffn_up_kernel, tiles_per_seq=tiles_per_seq, rs_s=rs_s),
        grid=(n_tiles, m // tm),
        in_specs=[
            pl.BlockSpec((tm, D_MODEL), lambda f, i: (i, 0)),
            pl.BlockSpec((ms, D_MODEL), lambda f, i: (0, 0)),
            pl.BlockSpec((None, D_MODEL, tf), lambda f, i: (li, 0, f)),
            pl.BlockSpec((None, D_MODEL, tf), lambda f, i: (li, 0, f)),
            pl.BlockSpec((None, slab, D_MODEL), lambda f, i: (li, f, 0)),
            pl.BlockSpec((None, CONV_W, tf), lambda f, i: (li, 0, f)),
            pl.BlockSpec((None, 1, tf), lambda f, i: (li, 0, f)),
            pl.BlockSpec((cr_s, tf), col),
        ],
        out_specs=[pl.BlockSpec((tm, tf), lambda f, i: (i, f)),
                   pl.BlockSpec((ms, tf), col),
                   pl.BlockSpec((1, cr, tf), lambda f, i: (i // tiles_per_seq, 0, f)),
                   pl.BlockSpec((cr_s, tf), col),
                   pl.BlockSpec((slab, D_MODEL), lambda f, i: (f, 0))],
        out_shape=[jax.ShapeDtypeStruct((m, D_FF), BF16), jax.ShapeDtypeStruct((ms, D_FF), BF16),
                   jax.ShapeDtypeStruct((nseq, cr, D_FF), F32), jax.ShapeDtypeStruct((cr_s, D_FF), F32),
                   jax.ShapeDtypeStruct((D_FF, D_MODEL), BF16)],
        scratch_shapes=[pltpu.VMEM((D_MODEL, 2 * tf), BF16), pltpu.VMEM((cr, tf), F32), pltpu.VMEM((cr_s, tf), F32)],
        compiler_params=_params(("arbitrary", "arbitrary")),
        name="ffn_up",
    )(h, hs, wa, wg, wd, conv_w, conv_b.reshape(DEPTH, 1, D_FF), init_s)


def _s5_prep_kernel(lam_ref, lam_x_ref, b_ref, c_ref, lbr_ref, lbi_ref, bdr_ref, bdi_ref, cdr_ref, cdi_ref):
    def lam_bar(lre, lim, log_step):
        delta = jnp.exp(log_step)
        mag = jnp.exp(lre * delta)
        return mag * jnp.cos(lim * delta), mag * jnp.sin(lim * delta)

    lbr, lbi = lam_bar(lam_ref[0], lam_ref[1], lam_ref[2])
    lbr_ref[...] = lbr
    lbi_ref[...] = lbi
    lre, lim = lam_x_ref[0], lam_x_ref[1]
    xr, xi = lam_bar(lre, lim, lam_x_ref[2])
    nr, ni = xr - 1.0, xi
    den = lre * lre + lim * lim
    cr = (nr * lre + ni * lim) / den
    ci = (ni * lre - nr * lim) / den
    br, bi = b_ref[0], b_ref[1]

    def spread(w, group_rows, group_cols):
        rows, width = w.shape
        cols = (rows // group_rows) * group_cols
        idx = lambda shp, d: lax.broadcasted_iota(jnp.int32, shp, d)
        repeat = (idx((width, cols), 1) % width == idx((width, cols), 0)).astype(BF16)
        own = idx((rows, cols), 0) // group_rows == idx((rows, cols), 1) // group_cols
        tiled = jnp.dot(w.astype(BF16), repeat, preferred_element_type=F32)
        return jnp.where(own, tiled, 0.0)

    bdr_ref[0] = spread(cr * br - ci * bi, SSM_GROUP_CH, SSM_P).astype(BF16)
    bdi_ref[0] = spread(cr * bi + ci * br, SSM_GROUP_CH, SSM_P).astype(BF16)
    cdr_ref[0] = spread(c_ref[0], SSM_GROUP_CH, SSM_P).T.astype(BF16)
    cdi_ref[0] = spread(c_ref[1], SSM_GROUP_CH, SSM_P).T.astype(BF16)


def s5_prep(lam_re, lam_im, log_step, b_re, b_im, c_re, c_im):
    gpb = SSM_GROUPS // SSM_BLOCKS
    lam = jnp.stack([lam_re, lam_im, jnp.broadcast_to(log_step[:, None], (SSM_GROUPS, SSM_P))])
    lam_x = jnp.repeat(lam, SSM_GROUP_CH, axis=1)
    b = jnp.stack([b_re, b_im]).transpose(0, 1, 3, 2).reshape(2, SSM_GROUPS * SSM_GROUP_CH, SSM_P)
    c = jnp.stack([c_re, c_im]).reshape(2, SSM_GROUPS * SSM_GROUP_CH, SSM_P)
    blk = lambda rows, width: pl.BlockSpec((rows, width), lambda k: (k, 0))
    stacked = lambda n, rows, width: pl.BlockSpec((n, rows, width), lambda k: (0, k, 0))
    mat = lambda rows, cols: pl.BlockSpec((1, rows, cols), lambda k: (k, 0, 0))
    return pl.pallas_call(
        _s5_prep_kernel,
        grid=(SSM_BLOCKS,),
        in_specs=[stacked(3, gpb, SSM_P), stacked(3, SSM_CH_BLOCK, SSM_P), stacked(2, SSM_CH_BLOCK, SSM_P),
                  stacked(2, SSM_CH_BLOCK, SSM_P)],
        out_specs=[blk(gpb, SSM_P), blk(gpb, SSM_P),
                   mat(SSM_CH_BLOCK, SSM_LANE_BLOCK), mat(SSM_CH_BLOCK, SSM_LANE_BLOCK),
                   mat(SSM_LANE_BLOCK, SSM_CH_BLOCK), mat(SSM_LANE_BLOCK, SSM_CH_BLOCK)],
        out_shape=[jax.ShapeDtypeStruct((SSM_GROUPS, SSM_P), F32)] * 2
        + [jax.ShapeDtypeStruct((SSM_BLOCKS, SSM_CH_BLOCK, SSM_LANE_BLOCK), BF16)] * 2
        + [jax.ShapeDtypeStruct((SSM_BLOCKS, SSM_LANE_BLOCK, SSM_CH_BLOCK), BF16)] * 2,
        compiler_params=_params(("parallel",)),
        name="s5_prep",
    )(lam, lam_x, b, c)


def _s5_dense_kernel(x_ref, gpre_ref, d_ref, lbr_ref, lbi_ref, bdr_ref, bdi_ref, cdr_ref, cdi_ref, s0r_ref, s0i_ref,
                     z_ref, sr_ref, si_ref, ur_ref, ui_ref, *, nb, tt):
    h = _rms(x_ref[...], gpre_ref[...])
    hb = h.astype(BF16)
    for kb in range(SSM_BLOCKS):
        ch = slice(kb * SSM_CH_BLOCK, (kb + 1) * SSM_CH_BLOCK)
        ln = slice(kb * SSM_LANE_BLOCK, (kb + 1) * SSM_LANE_BLOCK)
        ur_ref[...] = jnp.dot(hb[:, ch], bdr_ref[kb], preferred_element_type=F32)
        ui_ref[...] = jnp.dot(hb[:, ch], bdi_ref[kb], preferred_element_type=F32)
        a_re = jnp.broadcast_to(lbr_ref[:, ln], (nb, SSM_LANE_BLOCK))
        a_im = jnp.broadcast_to(lbi_ref[:, ln], (nb, SSM_LANE_BLOCK))
        s_re, s_im = s0r_ref[:, ln], s0i_ref[:, ln]
        for t in range(tt):
            rows = slice(t * nb, (t + 1) * nb)
            s_re, s_im = ((a_re * s_re - a_im * s_im) + ur_ref[rows, :],
                          (a_re * s_im + a_im * s_re) + ui_ref[rows, :])
            ur_ref[rows, :] = s_re
            ui_ref[rows, :] = s_im
        sr_ref[:, ln] = s_re
        si_ref[:, ln] = s_im
        y = (jnp.dot(ur_ref[...].astype(BF16), cdr_ref[kb], preferred_element_type=F32)
             - jnp.dot(ui_ref[...].astype(BF16), cdi_ref[kb], preferred_element_type=F32))
        y = y + d_ref[:, ch] * h[:, ch]
        z_ref[:, ch] = jax.nn.gelu(y).astype(z_ref.dtype)


def s5_dense(x, g_pre, d_skip, lbr, lbi, bdr, bdi, cdr, cdi, s0r, s0i, nb, tt):
    rows = nb * tt
    full = lambda shp: pl.BlockSpec(shp, lambda: tuple(0 for _ in shp))
    bd = full((SSM_BLOCKS, SSM_CH_BLOCK, SSM_LANE_BLOCK))
    cd = full((SSM_BLOCKS, SSM_LANE_BLOCK, SSM_CH_BLOCK))
    return pl.pallas_call(
        functools.partial(_s5_dense_kernel, nb=nb, tt=tt),
        in_specs=[full((rows, D_MODEL)), full((1, D_MODEL)), full((1, D_MODEL)), full((1, SSM_STATE)),
                  full((1, SSM_STATE)), bd, bd, cd, cd, full((nb, SSM_STATE)), full((nb, SSM_STATE))],
        out_specs=[full((rows, D_MODEL)), full((nb, SSM_STATE)), full((nb, SSM_STATE))],
        out_shape=[jax.ShapeDtypeStruct((rows, D_MODEL), BF16),
                   jax.ShapeDtypeStruct((nb, SSM_STATE), F32), jax.ShapeDtypeStruct((nb, SSM_STATE), F32)],
        scratch_shapes=[pltpu.VMEM((rows, SSM_LANE_BLOCK), F32), pltpu.VMEM((rows, SSM_LANE_BLOCK), F32)],
        compiler_params=pltpu.CompilerParams(vmem_limit_bytes=VMEM_LIMIT),
        name="s5_dense",
    )(x, g_pre.reshape(1, D_MODEL), d_skip.reshape(1, D_MODEL), lbr.reshape(1, SSM_STATE), lbi.reshape(1, SSM_STATE),
      bdr, bdi, cdr, cdi, s0r, s0i)


def _s5_seq_kernel(x_ref, gpre_ref, d_ref, lam_r_ref, lam_i_ref, bdr_ref, bdi_ref, cdr_ref, cdi_ref,
                   z_ref, sr_ref, si_ref, ur_ref, ui_ref, hs_ref, yo_ref):
    tt, pitch = S5_TT, S5_PITCH
    ti = pl.program_id(1)

    @pl.when(ti == 0)
    def _():
        sr_ref[...] = jnp.zeros(sr_ref.shape, F32)
        si_ref[...] = jnp.zeros(si_ref.shape, F32)

    h = _rms(x_ref[...], gpre_ref[...])
    zeros8 = jnp.zeros((SUBLANES, D_MODEL), F32)
    hs_ref[0:SUBLANES, :] = zeros8
    hs_ref[SUBLANES + tt:2 * SUBLANES + tt, :] = zeros8
    hs_ref[SUBLANES:SUBLANES + tt, :] = h
    hb = h.astype(BF16)
    hb_shift = hs_ref[4:tt + 12, :].astype(BF16)

    def even_rows(kb, lt):
        return pl.ds((kb * SUBLANES + 2 * lt) * pitch, tt)

    def odd_rows(kb, lt):
        return pl.ds((kb * SUBLANES + 2 * lt + 1) * pitch - 4, tt + SUBLANES)

    for kb in range(SSM_BLOCKS):
        ch = slice(kb * SSM_CH_BLOCK, (kb + 1) * SSM_CH_BLOCK)
        for u_ref, bd_ref in ((ur_ref, bdr_ref), (ui_ref, bdi_ref)):
            ue = jnp.dot(hb[:, ch], bd_ref[kb, :, 0:SSM_HALF], preferred_element_type=F32)
            uo = jnp.dot(hb_shift[:, ch], bd_ref[kb, :, SSM_HALF:SSM_LANE_BLOCK], preferred_element_type=F32)
            for lt in range(4):
                u_ref[even_rows(kb, lt), :] = ue[:, lt * LANES:(lt + 1) * LANES]
                u_ref[odd_rows(kb, lt), :] = uo[:, lt * LANES:(lt + 1) * LANES]

    blk = lambda q: slice(q * SUBLANES, (q + 1) * SUBLANES)
    a_re = [lam_r_ref[blk(q), :] for q in range(SSM_BLOCKS)]
    a_im = [lam_i_ref[blk(q), :] for q in range(SSM_BLOCKS)]

    def step(t, carry):
        new = []
        for q in range(SSM_BLOCKS):
            s_re, s_im = carry[2 * q], carry[2 * q + 1]
            rows = pl.ds(q * SUBLANES * pitch + t, SUBLANES, stride=pitch)
            n_re = (a_re[q] * s_re - a_im[q] * s_im) + ur_ref[rows, :]
            n_im = (a_re[q] * s_im + a_im[q] * s_re) + ui_ref[rows, :]
            ur_ref[rows, :] = n_re
            ui_ref[rows, :] = n_im
            new += [n_re, n_im]
        return tuple(new)

    init = []
    for q in range(SSM_BLOCKS):
        init += [sr_ref[0, blk(q), :], si_ref[0, blk(q), :]]
    final = lax.fori_loop(0, tt, step, tuple(init), unroll=2)
    for q in range(SSM_BLOCKS):
        sr_ref[0, blk(q), :] = final[2 * q]
        si_ref[0, blk(q), :] = final[2 * q + 1]

    for kb in range(SSM_BLOCKS):
        ch = slice(kb * SSM_CH_BLOCK, (kb + 1) * SSM_CH_BLOCK)

        def gather(u_ref, rows_of):
            return jnp.concatenate([u_ref[rows_of(kb, lt), :] for lt in range(4)], axis=1).astype(BF16)

        y_even = (jnp.dot(gather(ur_ref, even_rows), cdr_ref[kb, 0:SSM_HALF, :], preferred_element_type=F32)
                  - jnp.dot(gather(ui_ref, even_rows), cdi_ref[kb, 0:SSM_HALF, :], preferred_element_type=F32))
        yo_ref[...] = (jnp.dot(gather(ur_ref, odd_rows), cdr_ref[kb, SSM_HALF:SSM_LANE_BLOCK, :],
                               preferred_element_type=F32)
                       - jnp.dot(gather(ui_ref, odd_rows), cdi_ref[kb, SSM_HALF:SSM_LANE_BLOCK, :],
                                 preferred_element_type=F32))
        y = y_even + yo_ref[4:tt + 4, :]
        y = y + d_ref[:, ch] * h[:, ch]
        z_ref[:, ch] = jax.nn.gelu(y).astype(z_ref.dtype)


def s5_seq(x, g_pre, d_skip, lam_r, lam_i, bdr, bdi, cdr, cdi, nseq):
    m = x.shape[0]
    ntt = m // (nseq * S5_TT)
    fixed2 = lambda shp: pl.BlockSpec(shp, lambda si, ti: (0, 0))
    state = pl.BlockSpec((1, SSM_SLABS, LANES), lambda si, ti: (si, 0, 0))
    bd = _resident((SSM_BLOCKS, SSM_CH_BLOCK, SSM_LANE_BLOCK), lambda si, ti: (0, 0, 0))
    cd = _resident((SSM_BLOCKS, SSM_LANE_BLOCK, SSM_CH_BLOCK), lambda si, ti: (0, 0, 0))
    return pl.pallas_call(
        _s5_seq_kernel,
        grid=(nseq, ntt),
        in_specs=[pl.BlockSpec((S5_TT, D_MODEL), lambda si, ti: (si * ntt + ti, 0)),
                  fixed2((1, D_MODEL)), fixed2((1, D_MODEL)), fixed2((SSM_SLABS, LANES)), fixed2((SSM_SLABS, LANES)),
                  bd, bd, cd, cd],
        out_specs=[pl.BlockSpec((S5_TT, D_MODEL), lambda si, ti: (si * ntt + ti, 0)), state, state],
        out_shape=[jax.ShapeDtypeStruct((m, D_MODEL), BF16),
                   jax.ShapeDtypeStruct((nseq, SSM_SLABS, LANES), F32),
                   jax.ShapeDtypeStruct((nseq, SSM_SLABS, LANES), F32)],
        scratch_shapes=[pltpu.VMEM((SSM_SLABS * S5_PITCH, LANES), F32), pltpu.VMEM((SSM_SLABS * S5_PITCH, LANES), F32),
                        pltpu.VMEM((S5_TT + 2 * SUBLANES, D_MODEL), F32),
                        pltpu.VMEM((S5_TT + SUBLANES, SSM_CH_BLOCK), F32)],
        compiler_params=_params(("parallel", "arbitrary")),
        name="s5_seq",
    )(x, g_pre.reshape(1, D_MODEL), d_skip.reshape(1, D_MODEL), lam_r, lam_i, bdr, bdi, cdr, cdi)


def _to_slabs(v):
    lead = v.shape[:-1]
    v = v.reshape(lead + (SSM_BLOCKS, 2, SUBLANES // 2, LANES))
    return v.swapaxes(-3, -2).reshape(lead + (SSM_SLABS, LANES))


def _from_slabs(s):
    lead = s.shape[:-2]
    s = s.reshape(lead + (SSM_BLOCKS, SUBLANES // 2, 2, LANES))
    return s.swapaxes(-3, -2).reshape(lead + (SSM_STATE,))


def _slopes():
    return 2.0 ** (-8.0 * jnp.arange(1, SWA_Q_HEADS + 1, dtype=F32) / SWA_Q_HEADS)


def kernel(x_prompt, x_sample, state_ret, cache_swa_k, cache_swa_v, state_ssm_re, state_ssm_im, state_ffn_conv, norm_mix_pre, norm_mix_post, norm_ffn_pre, norm_ffn_post, w_in_even, w_out_even, swa_sinks, ssm_lam_re, ssm_lam_im, ssm_log_step, ssm_b_re, ssm_b_im, ssm_c_re, ssm_c_im, ssm_d, w_glu, ffn_w_a, ffn_w_g, ffn_conv_w, ffn_conv_b, ffn_w_down):
    pb, pl_len, _ = x_prompt.shape
    sb, sl_len, _ = x_sample.shape
    mp = pb * pl_len
    ms = sb * sl_len
    s_pad = BF16_ROWS
    carry_rows_s = (CONV_W - 1) * sb
    slopes = _slopes()

    xp = x_prompt.reshape(mp, D_MODEL)
    xs = x_sample.transpose(1, 0, 2).reshape(ms, D_MODEL)

    hp = rmsnorm_bf16(xp, norm_mix_pre[0], NORM_TM)
    hs = rmsnorm_bf16(xs, norm_mix_pre[0], ms)

    assert state_ret.shape[0] >= 2
    zero_ret = jnp.zeros((1, pb, RET_HEADS, RET_DK, RET_DV), F32)
    ret_p, ret_s, wk_p, wk_s, wv_p, wv_s = [], [], [], [], [], []
    sre_p, sre_s, sim_p, sim_s, conv_p, conv_s = [], [], [], [], [], []
    w_glu_b = None

    for layer in range(DEPTH):
        i = layer // 2
        if layer % 2 == 0:
            proj, proj_s, w_out_b = in_proj(hp, hs, w_in_even, w_out_even, i, IN_PROJ_TM, IN_PROJ_TN)
            proj = proj.reshape(pb, pl_len, EVEN_IN)
            last_even = i == state_ret.shape[0] - 1
            o_ret, s_ret = retention(proj, zero_ret, 0, ret_p if last_even else [], RET_CHUNK, RET_CHUNK, 1,
                                     RET_CHUNKS_PER_STEP)
            ret_p = s_ret if last_even else ret_p + [s_ret]
            kcol, vcol = KV_COL // LANES, KV_COL // LANES + 1
            blk = lambda c, prev: pl.BlockSpec(
                (1, SWA_BLOCK, LANES),
                (lambda bi, qi: (bi, jnp.maximum(qi - 1, 0), c)) if prev else (lambda bi, qi: (bi, qi, c)))
            o_swa = swa_attention(
                slopes, swa_sinks[i], proj,
                pl.BlockSpec((1, SWA_BLOCK, SWA_WIDTH), lambda bi, qi: (bi, qi, Q_COL // SWA_WIDTH)),
                [proj, proj, proj, proj], [blk(kcol, True), blk(vcol, True), blk(kcol, False), blk(vcol, False)],
                pb, 1, 1, pl_len // SWA_BLOCK, SWA_BLOCK, 0, SWA_BLOCK)
            kv_shape = (pb, WINDOW, SWA_KV_HEADS, SWA_HD)
            wk_p.append(proj[:, pl_len - WINDOW:, KV_COL:KV_COL + KV_WIDTH].reshape(kv_shape))
            wv_p.append(proj[:, pl_len - WINDOW:, KV_COL + KV_WIDTH:KV_COL + 2 * KV_WIDTH].reshape(kv_shape))

            proj_bt = proj_s.reshape(sl_len, sb, EVEN_IN).transpose(1, 0, 2)
            proj_pad = jnp.pad(proj_bt, ((0, 0), (0, s_pad - sl_len), (0, 0)))
            o_ret_s, s_ret_s = retention(proj_pad, state_ret, i, ret_s if last_even else [], sl_len, s_pad,
                                         SAMPLE_SEQS_PER_STEP, 1)
            ret_s = s_ret_s if last_even else ret_s + [s_ret_s]
            win = cache_swa_k.shape[2]
            n_even = cache_swa_k.shape[0]
            ck = cache_swa_k.reshape(n_even, sb, win, KV_WIDTH)
            cv = cache_swa_v.reshape(n_even, sb, win, KV_WIDTH)
            cache_spec = pl.BlockSpec((None, SAMPLE_SEQS_PER_STEP, win, KV_WIDTH), lambda bi, qi: (i, bi, 0, 0))
            cur = lambda c: pl.BlockSpec((SAMPLE_SEQS_PER_STEP, s_pad, LANES), lambda bi, qi: (bi, 0, c))
            o_swa_s = swa_attention(
                slopes, swa_sinks[i], proj_pad,
                pl.BlockSpec((SAMPLE_SEQS_PER_STEP, s_pad, SWA_WIDTH), lambda bi, qi: (bi, 0, Q_COL // SWA_WIDTH)),
                [ck, cv, proj_pad, proj_pad], [cache_spec, cache_spec, cur(kcol), cur(vcol)],
                sb, SAMPLE_SEQS_PER_STEP, SWA_CHAINS, 1, s_pad, win, 0)
            k_new = proj_bt[:, :, KV_COL:KV_COL + KV_WIDTH]
            v_new = proj_bt[:, :, KV_COL + KV_WIDTH:KV_COL + 2 * KV_WIDTH]
            kv_shape_s = (sb, win, SWA_KV_HEADS, SWA_HD)
            wk_s.append(jnp.concatenate([ck[i, :, sl_len:], k_new], axis=1).reshape(kv_shape_s))
            wv_s.append(jnp.concatenate([cv[i, :, sl_len:], v_new], axis=1).reshape(kv_shape_s))
            tb = lambda o: o[:, :sl_len].transpose(1, 0, 2).reshape(ms, -1)

            xp, xs, hp, hs, _ = proj_residual(
                [o_ret.reshape(mp, RET_WIDTH), o_swa.reshape(mp, SWA_WIDTH)], [tb(o_ret_s), tb(o_swa_s)],
                w_out_b, [0, 1], False, xp, xs, norm_mix_post[layer], norm_ffn_pre[layer], OUT_PROJ_TM,
                OUT_PROJ_ROW_PARTS, "out_proj")
        else:
            lbr, lbi, bdr, bdi, cdr, cdi = s5_prep(ssm_lam_re[i], ssm_lam_im[i], ssm_log_step[i], ssm_b_re[i],
                                                   ssm_b_im[i], ssm_c_re[i], ssm_c_im[i])
            lam_r = _to_slabs(lbr.reshape(SSM_STATE))
            lam_i = _to_slabs(lbi.reshape(SSM_STATE))
            zp, s_re, s_im = s5_seq(xp, norm_mix_pre[layer], ssm_d[i], lam_r, lam_i, bdr, bdi, cdr, cdi, pb)
            sre_p.append(_from_slabs(s_re).reshape(pb, SSM_GROUPS, SSM_P))
            sim_p.append(_from_slabs(s_im).reshape(pb, SSM_GROUPS, SSM_P))
            zs, s_re_s, s_im_s = s5_dense(xs, norm_mix_pre[layer], ssm_d[i], lbr, lbi, bdr, bdi, cdr, cdi,
                                          state_ssm_re[i].reshape(sb, SSM_STATE),
                                          state_ssm_im[i].reshape(sb, SSM_STATE), sb, sl_len)
            sre_s.append(s_re_s.reshape(sb, SSM_GROUPS, SSM_P))
            sim_s.append(s_im_s.reshape(sb, SSM_GROUPS, SSM_P))
            xp, xs, hp, hs, _ = proj_residual([zp], [zs], w_glu_b, [0], True, xp, xs, norm_mix_post[layer],
                                              norm_ffn_pre[layer], GLU_TM, GLU_ROW_PARTS, "glu_proj")

        init_s = state_ffn_conv[layer].transpose(1, 0, 2).reshape(carry_rows_s, D_FF)
        act, act_s, tail, tail_s, w_down_b = ffn_up(hp, hs, ffn_w_a, ffn_w_g, ffn_w_down, layer, ffn_conv_w,
                                                    ffn_conv_b, init_s, sb, FFN_UP_TM, FFN_TF, pl_len)
        conv_p.append(tail[:, SUBLANES - (CONV_W - 1):])
        conv_s.append(tail_s.reshape(CONV_W - 1, sb, D_FF).transpose(1, 0, 2))
        next_even = layer + 1 < DEPTH and (layer + 1) % 2 == 0
        next_odd = layer + 1 < DEPTH and (layer + 1) % 2 == 1
        xp, xs, hp, hs, w_cast = proj_residual(
            [act], [act_s], w_down_b, [0], False, xp, xs, norm_ffn_post[layer],
            norm_mix_pre[layer + 1] if next_even else None, FFN_DOWN_TM, FFN_DOWN_ROW_PARTS, "ffn_down",
            cast=(w_glu, (layer + 1) // 2) if next_odd else None)
        if next_odd:
            w_glu_b = w_cast

    y_prompt = xp.reshape(pb, pl_len, D_MODEL)
    y_sample = xs.reshape(sl_len, sb, D_MODEL).transpose(1, 0, 2)
    return (y_prompt, y_sample, ret_p, ret_s, jnp.stack(wk_p), jnp.stack(wk_s),
            jnp.stack(wv_p), jnp.stack(wv_s), jnp.stack(sre_p), jnp.stack(sre_s), jnp.stack(sim_p),
            jnp.stack(sim_s), jnp.stack(conv_p), jnp.stack(conv_s))
```

```python
import functools

import jax
import jax.numpy as jnp
from jax import lax
from jax.experimental import pallas as pl
from jax.experimental.pallas import tpu as pltpu

F32 = jnp.float32
BF16 = jnp.bfloat16

D_MODEL = 2048
DEPTH = 4
RET_HEADS = 8
RET_DK = 128
RET_DV = 128
RET_CHUNK = 128
RET_WIDTH = RET_HEADS * RET_DV
SWA_Q_HEADS = 16
SWA_KV_HEADS = 2
SWA_HD = 64
WINDOW = 128
SWA_BLOCK = 128
SWA_WIDTH = SWA_Q_HEADS * SWA_HD
SWA_CHAINS = 4
RET_CHUNKS_PER_STEP = 4
SAMPLE_SEQS_PER_STEP = 4

NORM_TM = 512
IN_PROJ_TM, IN_PROJ_TN = 1024, 768
W_OUT_CAST_ROWS = 512
OUT_PROJ_TM, OUT_PROJ_ROW_PARTS = 512, 2
GLU_TM, GLU_ROW_PARTS = 256, 2
FFN_UP_TM, FFN_TF = 1024, 512
FFN_DOWN_TM, FFN_DOWN_ROW_PARTS = 256, 1
EVEN_IN = 5376
Q_COL = RET_WIDTH * 4
KV_COL = Q_COL + SWA_WIDTH
KV_WIDTH = SWA_KV_HEADS * SWA_HD
SSM_GROUP_CH = 16
SSM_GROUPS = 128
SSM_P = 64
SSM_STATE = SSM_GROUPS * SSM_P
SSM_LANE_BLOCK = 1024
SSM_CH_BLOCK = 256
SSM_BLOCKS = SSM_STATE // SSM_LANE_BLOCK
SSM_HALF = SSM_LANE_BLOCK // 2
D_FF = 5632
CONV_W = 3
NORM_EPS = 1e-6

LANES = 128
SUBLANES = 8
BF16_ROWS = 16
VMEM_LIMIT = 56 * 1024 * 1024

SSM_SLABS = SSM_STATE // LANES
S5_TT = 256
S5_PITCH = S5_TT + 4


def _params(sem, vmem=VMEM_LIMIT):
    return pltpu.CompilerParams(dimension_semantics=sem, vmem_limit_bytes=vmem)


def _rms(x, g):
    return x * lax.rsqrt(jnp.mean(x * x, axis=-1, keepdims=True) + NORM_EPS) * g


def _resident(shape, index_map):
    return pl.BlockSpec(shape, index_map, pipeline_mode=pl.Buffered(1))


def _norm_kernel(x_ref, g_ref, o_ref):
    o_ref[...] = _rms(x_ref[...], g_ref[...]).astype(o_ref.dtype)


def rmsnorm_bf16(x, g, tm):
    m, d = x.shape
    return pl.pallas_call(
        _norm_kernel,
        grid=(m // tm,),
        in_specs=[pl.BlockSpec((tm, d), lambda i: (i, 0)), pl.BlockSpec((1, d), lambda i: (0, 0))],
        out_specs=pl.BlockSpec((tm, d), lambda i: (i, 0)),
        out_shape=jax.ShapeDtypeStruct((m, d), BF16),
        compiler_params=_params(("parallel",)),
        name="rmsnorm",
    )(x, g.reshape(1, d))


def _in_proj_kernel(x_ref, xs_ref, w_ref, wo_ref, o_ref, os_ref, wob_ref, wb_ref):
    @pl.when(pl.program_id(1) == 0)
    def _():
        wb_ref[...] = w_ref[...].astype(BF16)
        wob_ref[...] = wo_ref[...].astype(BF16)
        os_ref[...] = jnp.dot(xs_ref[...], wb_ref[...], preferred_element_type=F32)

    o_ref[...] = jnp.dot(x_ref[...], wb_ref[...], preferred_element_type=F32)


def in_proj(h, hs, w_in, w_out, li, tm, tn):
    m, ms = h.shape[0], hs.shape[0]
    n_tiles = EVEN_IN // tn
    slab = W_OUT_CAST_ROWS
    n_slabs = D_MODEL // slab
    assert n_slabs <= n_tiles
    slab_idx = lambda j, i: (li, jnp.minimum(j, n_slabs - 1), 0)
    return pl.pallas_call(
        _in_proj_kernel,
        grid=(n_tiles, m // tm),
        in_specs=[pl.BlockSpec((tm, D_MODEL), lambda j, i: (i, 0)),
                  pl.BlockSpec((ms, D_MODEL), lambda j, i: (0, 0)),
                  pl.BlockSpec((None, D_MODEL, tn), lambda j, i: (li, 0, j)),
                  pl.BlockSpec((None, slab, D_MODEL), slab_idx)],
        out_specs=[pl.BlockSpec((tm, tn), lambda j, i: (i, j)),
                   pl.BlockSpec((ms, tn), lambda j, i: (0, j)),
                   pl.BlockSpec((slab, D_MODEL), lambda j, i: (jnp.minimum(j, n_slabs - 1), 0))],
        out_shape=[jax.ShapeDtypeStruct((m, EVEN_IN), F32), jax.ShapeDtypeStruct((ms, EVEN_IN), F32),
                   jax.ShapeDtypeStruct((D_MODEL, D_MODEL), BF16)],
        scratch_shapes=[pltpu.VMEM((D_MODEL, tn), BF16)],
        compiler_params=_params(("arbitrary", "arbitrary")),
        name="in_proj",
    )(h, hs, w_in, w_out)


def _retention_kernel(decay_ref, q_ref, k_ref, v_ref, g_ref, intra_ref, read_ref, write_ref, s0_ref, *rest,
                      nseq, chunks, c):
    prev_refs, o_ref, out_ref = rest[:-2], rest[-2], rest[-1]
    s_ref = out_ref.at[len(prev_refs)] if prev_refs else out_ref
    ci = pl.program_id(1)

    @pl.when(ci == 0)
    def _():
        s_ref[...] = s0_ref[...]
        for j, prev_ref in enumerate(prev_refs):
            out_ref[j] = prev_ref[...]

    heads = [slice(h * RET_DK, (h + 1) * RET_DK) for h in range(RET_HEADS)]
    nt = (((1,), (1,)), ((), ()))
    tn = (((0,), (0,)), ((), ()))
    jobs = [(b, h) for b in range(nseq) for h in range(RET_HEADS)]
    for cc in range(chunks):
        rows = slice(cc * c, (cc + 1) * c)
        first = []
        for b, h in jobs:
            sl = heads[h]
            q = q_ref[b, rows, sl].astype(BF16)
            k = k_ref[b, rows, sl] * (RET_DK ** -0.5)
            v = v_ref[b, rows, sl].astype(BF16)
            s = s_ref[b, h]
            sc = lax.dot_general(q, k.astype(BF16), nt, preferred_element_type=F32)
            qs = jnp.dot(q, s.astype(BF16), preferred_element_type=F32)
            kv = lax.dot_general((k * write_ref[h]).astype(BF16), v, tn, preferred_element_type=F32)
            first.append((v, s, sc, qs, kv))
        outs = []
        for (b, h), (v, s, sc, qs, kv) in zip(jobs, first):
            outs.append(jnp.dot((sc * intra_ref[h]).astype(BF16), v, preferred_element_type=F32) + qs * read_ref[h])
            s_ref[b, h] = s * decay_ref[h] + kv
        for (b, h), o in zip(jobs, outs):
            g = g_ref[b, rows, heads[h]]
            o = o * lax.rsqrt(jnp.mean(o * o, axis=-1, keepdims=True) + NORM_EPS)
            o_ref[b, rows, heads[h]] = (o * (g * jax.nn.sigmoid(g))).astype(o_ref.dtype)


def _retention_tables(c_real, c_pad):
    lg = jnp.log(1.0 - 2.0 ** (-5.0 - jnp.arange(RET_HEADS, dtype=F32)))
    idx = jnp.arange(c_pad, dtype=F32)
    diff = idx[:, None] - idx[None, :]
    intra = jnp.where(diff >= 0, jnp.exp(lg[:, None, None] * jnp.maximum(diff, 0.0)), 0.0)
    read = jnp.exp(lg[:, None] * (idx[None, :] + 1.0))
    write = jnp.exp(lg[:, None] * (c_real - 1.0 - idx[None, :]))
    decay = jnp.exp(lg * c_real)
    bshape = (RET_HEADS, c_pad, RET_DV)
    return decay, intra, jnp.broadcast_to(read[:, :, None], bshape), jnp.broadcast_to(write[:, :, None], bshape)


def retention(proj, s0, li, prev_states, c_real, c_pad, nseq, chunks):
    b, l, _ = proj.shape
    rows = chunks * c_pad
    decay, intra, read, write = _retention_tables(c_real, c_pad)

    def col(j):
        return pl.BlockSpec((nseq, rows, RET_WIDTH), lambda bi, ci: (bi, ci, j))

    state_shape = (nseq, RET_HEADS, RET_DK, RET_DV)
    state_spec = pl.BlockSpec(state_shape, lambda bi, ci: (bi, 0, 0, 0))
    out_state_shape = s0.shape[1:]
    out_state_spec = state_spec
    if prev_states:
        n_out = len(prev_states) + 1
        out_state_shape = (n_out,) + out_state_shape
        out_state_spec = pl.BlockSpec((n_out,) + state_shape, lambda bi, ci: (0, bi, 0, 0, 0))
    return pl.pallas_call(
        functools.partial(_retention_kernel, nseq=nseq, chunks=chunks, c=c_pad),
        grid=(b // nseq, l // rows),
        in_specs=[
            pl.BlockSpec(memory_space=pltpu.SMEM),
            col(0), col(1), col(2), col(3),
            pl.BlockSpec((RET_HEADS, c_pad, c_pad), lambda bi, ci: (0, 0, 0)),
            pl.BlockSpec((RET_HEADS, c_pad, RET_DV), lambda bi, ci: (0, 0, 0)),
            pl.BlockSpec((RET_HEADS, c_pad, RET_DV), lambda bi, ci: (0, 0, 0)),
            pl.BlockSpec((None,) + state_shape, lambda bi, ci: (li, bi, 0, 0, 0)),
        ] + [state_spec] * len(prev_states),
        out_specs=[pl.BlockSpec((nseq, rows, RET_WIDTH), lambda bi, ci: (bi, ci, 0)), out_state_spec],
        out_shape=[jax.ShapeDtypeStruct((b, l, RET_WIDTH), BF16), jax.ShapeDtypeStruct(out_state_shape, F32)],
        compiler_params=_params(("parallel", "arbitrary")),
        name="retention",
    )(decay, proj, proj, proj, proj, intra, read, write, s0, *prev_states)


def _swa_kernel(bias_p_ref, bias_c_ref, sink_ref, q_ref, kp_ref, vp_ref, kc_ref, vc_ref, o_ref, *,
                nseq, group, tq, q0_base, q0_step):
    prev_always_visible = q0_step == 0 and q0_base >= WINDOW
    if not prev_always_visible:
        q0 = q0_base + pl.program_id(1) * q0_step
        prev_penalty = jnp.where(q0 >= WINDOW, 0.0, -jnp.inf).astype(F32)
    lane_lo = lax.broadcasted_iota(jnp.int32, (tq, LANES), 1) < SWA_HD
    scale = SWA_HD ** -0.5
    nt = (((1,), (1,)), ((), ()))
    pairs_per_kv = SWA_Q_HEADS // SWA_KV_HEADS // 2
    lane_sum = lambda x: jnp.sum(x, axis=-1, keepdims=True)
    lane_max = lambda x: jnp.max(x, axis=-1, keepdims=True)

    def both(ref, b):
        x = ref[b]
        return x.astype(BF16), pltpu.roll(x, SWA_HD, 1).astype(BF16)

    keys = [(both(kp_ref, b), both(kc_ref, b)) for b in range(nseq)]
    vals = [(both(vp_ref, b), both(vc_ref, b)) for b in range(nseq)]
    all_chains = [(b, kv, parity) for b in range(nseq) for kv in range(SWA_KV_HEADS) for parity in range(2)]
    outs = {}
    for g0 in range(0, len(all_chains), group):
        chains = all_chains[g0:g0 + group]
        scores = []
        for b, kv, parity in chains:
            swapped = (kv == 0) != (parity == 0)
            (kp, kp_sw), (kc, kc_sw) = keys[b]
            kprev, kcur = (kp_sw, kc_sw) if swapped else (kp, kc)
            qs = []
            for p in range(pairs_per_kv):
                pair = kv * pairs_per_kv + p
                qp = q_ref[b, :, pair * LANES:(pair + 1) * LANES]
                qs.append(jnp.where(lane_lo, qp, 0.0) if parity == 0 else jnp.where(lane_lo, 0.0, qp))
            qh = jnp.concatenate(qs, axis=0).astype(BF16)
            scores.append((lax.dot_general(qh, kprev, nt, preferred_element_type=F32),
                           lax.dot_general(qh, kcur, nt, preferred_element_type=F32)))
        probs = []
        for (b, kv, parity), (sp, sc) in zip(chains, scores):
            chain = 2 * kv + parity
            sink = sink_ref[chain][:, :1]
            sp = sp * scale + bias_p_ref[chain]
            sc = sc * scale + bias_c_ref[chain]
            if not prev_always_visible:
                sp = sp + prev_penalty
            if tq == WINDOW:
                m = jnp.maximum(lane_max(jnp.maximum(sp, sc)), sink)
            else:
                m = jnp.maximum(jnp.maximum(lane_max(sp), lane_max(sc)), sink)
            pp = jnp.exp(sp - m)
            pc = jnp.exp(sc - m)
            psum = lane_sum(pp + pc) if tq == WINDOW else lane_sum(pp) + lane_sum(pc)
            denom = psum + jnp.exp(sink - m)
            probs.append(((pp / denom).astype(BF16), (pc / denom).astype(BF16)))
        for (b, kv, parity), (pp, pc) in zip(chains, probs):
            swapped = (kv == 0) != (parity == 0)
            (vp, vp_sw), (vc, vc_sw) = vals[b]
            vprev, vcur = (vp_sw, vc_sw) if swapped else (vp, vc)
            outs[b, kv, parity] = (jnp.dot(pp, vprev, preferred_element_type=F32)
                                   + jnp.dot(pc, vcur, preferred_element_type=F32))
        for b, kv, parity in chains:
            if parity == 0:
                continue
            for p in range(pairs_per_kv):
                pair = kv * pairs_per_kv + p
                rs = slice(p * tq, (p + 1) * tq)
                o_ref[b, :, pair * LANES:(pair + 1) * LANES] = jnp.where(lane_lo, outs[b, kv, 0][rs],
                                                                          outs[b, kv, 1][rs]).astype(o_ref.dtype)


def _swa_tables(slopes, sinks, tq):
    per = SWA_Q_HEADS // SWA_CHAINS

    def rows_of(vals):
        v = vals.astype(F32).reshape(SWA_KV_HEADS, per, 2).transpose(0, 2, 1).reshape(SWA_CHAINS, per)
        return jnp.broadcast_to(v[:, :, None, None], (SWA_CHAINS, per, tq, 1))

    r = jnp.arange(tq, dtype=jnp.int32)[:, None]
    dist_p = r - jnp.arange(WINDOW, dtype=jnp.int32)[None, :] + WINDOW
    dist_c = r - jnp.arange(tq, dtype=jnp.int32)[None, :]
    slope = rows_of(slopes)
    bias_p = jnp.where(dist_p <= WINDOW, -(slope * dist_p.astype(F32)), -jnp.inf)
    bias_c = jnp.where(dist_c >= 0, -(slope * dist_c.astype(F32)), -jnp.inf)
    sink = jnp.broadcast_to(rows_of(sinks), (SWA_CHAINS, per, tq, LANES))
    flat = lambda t: t.reshape(SWA_CHAINS, per * tq, t.shape[-1])
    return flat(bias_p), flat(bias_c), flat(sink)


def swa_attention(slopes, sinks, q_arr, q_spec, kv_arrs, kv_specs, b, nseq, group, nb, tq, q0_base, q0_step):
    assert q0_base % WINDOW == 0 and q0_step % WINDOW == 0
    rows = (SWA_Q_HEADS // SWA_CHAINS) * tq
    table = lambda width: pl.BlockSpec((SWA_CHAINS, rows, width), lambda bi, i: (0, 0, 0))
    return pl.pallas_call(
        functools.partial(_swa_kernel, nseq=nseq, group=group, tq=tq, q0_base=q0_base, q0_step=q0_step),
        grid=(b // nseq, nb),
        in_specs=[table(WINDOW), table(tq), table(LANES), q_spec] + kv_specs,
        out_specs=pl.BlockSpec((nseq, tq, SWA_WIDTH), lambda bi, i: (bi, i, 0)),
        out_shape=jax.ShapeDtypeStruct((b, nb * tq, SWA_WIDTH), BF16),
        compiler_params=_params(("parallel", "parallel")),
        name="swa",
    )(*_swa_tables(slopes, sinks, tq), q_arr, *kv_arrs)


def _res_tile(acts, ws, glu, x_ref, gpost_ref, gnext_ref, xo_ref, ho_ref, parts=1):
    rows = x_ref.shape[0] // parts
    ys = []
    for p in range(parts):
        rs = slice(p * rows, (p + 1) * rows)
        a = acts[0][rs, :] if len(acts) == 1 else jnp.concatenate([a_ref[rs, :] for a_ref in acts], axis=1)
        ys.append(jnp.dot(a, ws[0][...], preferred_element_type=F32))
    for p, y in enumerate(ys):
        rs = slice(p * rows, (p + 1) * rows)
        if glu:
            y = y[:, :D_MODEL] * jax.nn.sigmoid(y[:, D_MODEL:])
        x_new = x_ref[rs, :] + _rms(y, gpost_ref[...])
        xo_ref[rs, :] = x_new
        if ho_ref is not None:
            ho_ref[rs, :] = _rms(x_new, gnext_ref[...]).astype(ho_ref.dtype)


def _proj_res_kernel(*refs, n_pieces, glu, has_next, has_cast, parts):
    it = iter(refs)
    take = lambda k: [next(it) for _ in range(k)]
    acts, acts_s, ws = take(n_pieces), take(n_pieces), take(1)
    x_ref, xs_ref, gpost_ref = take(3)
    gnext_ref = next(it) if has_next else None
    src_ref = next(it) if has_cast else None
    xo_ref, xos_ref = take(2)
    ho_ref, hos_ref = take(2) if has_next else (None, None)
    dst_ref = next(it) if has_cast else None

    _res_tile(acts, ws, glu, x_ref, gpost_ref, gnext_ref, xo_ref, ho_ref, parts=parts)

    @pl.when(pl.program_id(0) == 0)
    def _():
        _res_tile(acts_s, ws, glu, xs_ref, gpost_ref, gnext_ref, xos_ref, hos_ref)

    if has_cast:
        dst_ref[...] = src_ref[...].astype(dst_ref.dtype)


def proj_residual(acts, acts_s, w, glu, x, xs, g_post, g_next, tm, parts, name, cast=None):
    m, ms = x.shape[0], xs.shape[0]
    steps = m // tm
    n_pieces = len(acts)
    has_next = g_next is not None
    row = lambda i: (i, 0)
    fixed = lambda i: (0, 0)
    in_specs = [pl.BlockSpec((tm, a.shape[1]), row) for a in acts]
    in_specs += [pl.BlockSpec((ms, a.shape[1]), fixed) for a in acts_s]
    args = list(acts) + list(acts_s)
    in_specs.append(_resident(w.shape, fixed))
    args.append(w)
    in_specs += [pl.BlockSpec((tm, D_MODEL), row), pl.BlockSpec((ms, D_MODEL), fixed), pl.BlockSpec((1, D_MODEL), fixed)]
    args += [x, xs, g_post.reshape(1, D_MODEL)]
    if has_next:
        in_specs.append(pl.BlockSpec((1, D_MODEL), fixed))
        args.append(g_next.reshape(1, D_MODEL))
    if cast is not None:
        src, li = cast
        slab = src.shape[1] // steps
        in_specs.append(pl.BlockSpec((None, slab, src.shape[2]), lambda i: (li, i, 0)))
        args.append(src)
    out_specs = [pl.BlockSpec((tm, D_MODEL), row), pl.BlockSpec((ms, D_MODEL), fixed)]
    out_shape = [jax.ShapeDtypeStruct((m, D_MODEL), F32), jax.ShapeDtypeStruct((ms, D_MODEL), F32)]
    if has_next:
        out_specs += [pl.BlockSpec((tm, D_MODEL), row), pl.BlockSpec((ms, D_MODEL), fixed)]
        out_shape += [jax.ShapeDtypeStruct((m, D_MODEL), BF16), jax.ShapeDtypeStruct((ms, D_MODEL), BF16)]
    if cast is not None:
        out_specs.append(pl.BlockSpec((slab, src.shape[2]), row))
        out_shape.append(jax.ShapeDtypeStruct(src.shape[1:], BF16))
    outs = pl.pallas_call(
        functools.partial(_proj_res_kernel, n_pieces=n_pieces, glu=glu, has_next=has_next, has_cast=cast is not None,
                          parts=parts),
        grid=(steps,),
        in_specs=in_specs,
        out_specs=out_specs,
        out_shape=out_shape,
        compiler_params=_params(("arbitrary",)),
        name=name,
    )(*args)
    outs = list(outs)
    x_new, xs_new = outs[0], outs[1]
    h_new, hs_new = (outs[2], outs[3]) if has_next else (None, None)
    w_cast = outs[-1] if cast is not None else None
    return x_new, xs_new, h_new, hs_new, w_cast


def _ffn_rows(x_ref, w_ref, cw_ref, cb_ref, carry_ref, act_ref, *, rs):
    tf = act_ref.shape[1]
    ag = jnp.dot(x_ref[...], w_ref[...], preferred_element_type=F32)
    a, g = ag[:, :tf], ag[:, tf:]
    cr = carry_ref.shape[0]
    ext = jnp.concatenate([carry_ref[...], a], axis=0)
    conv = cb_ref[...] + cw_ref[0:1, :] * pltpu.roll(ext, 2 * rs, 0)[cr:]
    conv = conv + cw_ref[1:2, :] * pltpu.roll(ext, rs, 0)[cr:]
    conv = conv + cw_ref[2:3, :] * a
    act_ref[...] = (jax.nn.gelu(conv) * g).astype(act_ref.dtype)
    carry_ref[...] = ext[a.shape[0]:]


def _ffn_up_kernel(x_ref, xs_ref, wa_ref, wg_ref, wd_ref, cw_ref, cb_ref, init_s_ref,
                   act_ref, act_s_ref, tail_ref, tail_s_ref, wdb_ref, w_ref, carry_ref, carry_s_ref, *,
                   tiles_per_seq, rs_s):
    i = pl.program_id(1)
    t_in_seq = i % tiles_per_seq
    tile = functools.partial(_ffn_rows, w_ref=w_ref, cw_ref=cw_ref, cb_ref=cb_ref)
    tf = act_ref.shape[1]

    @pl.when(i == 0)
    def _():
        w_ref[:, :tf] = wa_ref[...].astype(BF16)
        w_ref[:, tf:] = wg_ref[...].astype(BF16)
        wdb_ref[...] = wd_ref[...].astype(BF16)
        carry_s_ref[...] = init_s_ref[...]
        tile(xs_ref, carry_ref=carry_s_ref, act_ref=act_s_ref, rs=rs_s)
        tail_s_ref[...] = carry_s_ref[...]

    @pl.when(t_in_seq == 0)
    def _():
        carry_ref[...] = jnp.zeros(carry_ref.shape, F32)

    tile(x_ref, carry_ref=carry_ref, act_ref=act_ref, rs=1)

    @pl.when(t_in_seq == tiles_per_seq - 1)
    def _():
        tail_ref[0] = carry_ref[...]


def ffn_up(h, hs, wa, wg, wd, li, conv_w, conv_b, init_s, rs_s, tm, tf, rows_per_seq):
    m, ms = h.shape[0], hs.shape[0]
    cr = SUBLANES
    cr_s = init_s.shape[0]
    tiles_per_seq = rows_per_seq // tm
    nseq = m // rows_per_seq
    n_tiles = D_FF // tf
    slab = D_FF // n_tiles
    col = lambda f, i: (0, f)
    return pl.pallas_call(
        functools.partial(_ffn_up_kernel, tiles_per_seq=tiles_per_seq, rs_s=rs_s),
        grid=(n_tiles, m // tm),
        in_specs=[
            pl.BlockSpec((tm, D_MODEL), lambda f, i: (i, 0)),
            pl.BlockSpec((ms, D_MODEL), lambda f, i: (0, 0)),
            pl.BlockSpec((None, D_MODEL, tf), lambda f, i: (li, 0, f)),
            pl.BlockSpec((None, D_MODEL, tf), lambda f, i: (li, 0, f)),
            pl.BlockSpec((None, slab, D_MODEL), lambda f, i: (li, f, 0)),
            pl.BlockSpec((None, CONV_W, tf), lambda f, i: (li, 0, f)),
            pl.BlockSpec((None, 1, tf), lambda f, i: (li, 0, f)),
            pl.BlockSpec((cr_s, tf), col),
        ],
        out_specs=[pl.BlockSpec((tm, tf), lambda f, i: (i, f)),
                   pl.BlockSpec((ms, tf), col),
                   pl.BlockSpec((1, cr, tf), lambda f, i: (i // tiles_per_seq, 0, f)),
                   pl.BlockSpec((cr_s, tf), col),
                   pl.BlockSpec((slab, D_MODEL), lambda f, i: (f, 0))],
        out_shape=[jax.ShapeDtypeStruct((m, D_FF), BF16), jax.ShapeDtypeStruct((ms, D_FF), BF16),
                   jax.ShapeDtypeStruct((nseq, cr, D_FF), F32), jax.ShapeDtypeStruct((cr_s, D_FF), F32),
                   jax.ShapeDtypeStruct((D_FF, D_MODEL), BF16)],
        scratch_shapes=[pltpu.VMEM((D_MODEL, 2 * tf), BF16), pltpu.VMEM((cr, tf), F32), pltpu.VMEM((cr_s, tf), F32)],
        compiler_params=_params(("arbitrary", "arbitrary")),
        name="ffn_up",
    )(h, hs, wa, wg, wd, conv_w, conv_b.reshape(DEPTH, 1, D_FF), init_s)


def _s5_prep_kernel(lam_ref, lam_x_ref, b_ref, c_ref, lbr_ref, lbi_ref, bdr_ref, bdi_ref, cdr_ref, cdi_ref):
    def lam_bar(lre, lim, log_step):
        delta = jnp.exp(log_step)
        mag = jnp.exp(lre * delta)
        return mag * jnp.cos(lim * delta), mag * jnp.sin(lim * delta)

    lbr, lbi = lam_bar(lam_ref[0], lam_ref[1], lam_ref[2])
    lbr_ref[...] = lbr
    lbi_ref[...] = lbi
    lre, lim = lam_x_ref[0], lam_x_ref[1]
    xr, xi = lam_bar(lre, lim, lam_x_ref[2])
    nr, ni = xr - 1.0, xi
    den = lre * lre + lim * lim
    cr = (nr * lre + ni * lim) / den
    ci = (ni * lre - nr * lim) / den
    br, bi = b_ref[0], b_ref[1]

    def spread(w, group_rows, group_cols):
        rows, width = w.shape
        cols = (rows // group_rows) * group_cols
        idx = lambda shp, d: lax.broadcasted_iota(jnp.int32, shp, d)
        repeat = (idx((width, cols), 1) % width == idx((width, cols), 0)).astype(BF16)
        own = idx((rows, cols), 0) // group_rows == idx((rows, cols), 1) // group_cols
        tiled = jnp.dot(w.astype(BF16), repeat, preferred_element_type=F32)
        return jnp.where(own, tiled, 0.0)

    bdr_ref[0] = spread(cr * br - ci * bi, SSM_GROUP_CH, SSM_P).astype(BF16)
    bdi_ref[0] = spread(cr * bi + ci * br, SSM_GROUP_CH, SSM_P).astype(BF16)
    cdr_ref[0] = spread(c_ref[0], SSM_GROUP_CH, SSM_P).T.astype(BF16)
    cdi_ref[0] = spread(c_ref[1], SSM_GROUP_CH, SSM_P).T.astype(BF16)


def s5_prep(lam_re, lam_im, log_step, b_re, b_im, c_re, c_im):
    gpb = SSM_GROUPS // SSM_BLOCKS
    lam = jnp.stack([lam_re, lam_im, jnp.broadcast_to(log_step[:, None], (SSM_GROUPS, SSM_P))])
    lam_x = jnp.repeat(lam, SSM_GROUP_CH, axis=1)
    b = jnp.stack([b_re, b_im]).transpose(0, 1, 3, 2).reshape(2, SSM_GROUPS * SSM_GROUP_CH, SSM_P)
    c = jnp.stack([c_re, c_im]).reshape(2, SSM_GROUPS * SSM_GROUP_CH, SSM_P)
    blk = lambda rows, width: pl.BlockSpec((rows, width), lambda k: (k, 0))
    stacked = lambda n, rows, width: pl.BlockSpec((n, rows, width), lambda k: (0, k, 0))
    mat = lambda rows, cols: pl.BlockSpec((1, rows, cols), lambda k: (k, 0, 0))
    return pl.pallas_call(
        _s5_prep_kernel,
        grid=(SSM_BLOCKS,),
        in_specs=[stacked(3, gpb, SSM_P), stacked(3, SSM_CH_BLOCK, SSM_P), stacked(2, SSM_CH_BLOCK, SSM_P),
                  stacked(2, SSM_CH_BLOCK, SSM_P)],
        out_specs=[blk(gpb, SSM_P), blk(gpb, SSM_P),
                   mat(SSM_CH_BLOCK, SSM_LANE_BLOCK), mat(SSM_CH_BLOCK, SSM_LANE_BLOCK),
                   mat(SSM_LANE_BLOCK, SSM_CH_BLOCK), mat(SSM_LANE_BLOCK, SSM_CH_BLOCK)],
        out_shape=[jax.ShapeDtypeStruct((SSM_GROUPS, SSM_P), F32)] * 2
        + [jax.ShapeDtypeStruct((SSM_BLOCKS, SSM_CH_BLOCK, SSM_LANE_BLOCK), BF16)] * 2
        + [jax.ShapeDtypeStruct((SSM_BLOCKS, SSM_LANE_BLOCK, SSM_CH_BLOCK), BF16)] * 2,
        compiler_params=_params(("parallel",)),
        name="s5_prep",
    )(lam, lam_x, b, c)


def _s5_dense_kernel(x_ref, gpre_ref, d_ref, lbr_ref, lbi_ref, bdr_ref, bdi_ref, cdr_ref, cdi_ref, s0r_ref, s0i_ref,
                     z_ref, sr_ref, si_ref, ur_ref, ui_ref, *, nb, tt):
    h = _rms(x_ref[...], gpre_ref[...])
    hb = h.astype(BF16)
    for kb in range(SSM_BLOCKS):
        ch = slice(kb * SSM_CH_BLOCK, (kb + 1) * SSM_CH_BLOCK)
        ln = slice(kb * SSM_LANE_BLOCK, (kb + 1) * SSM_LANE_BLOCK)
        ur_ref[...] = jnp.dot(hb[:, ch], bdr_ref[kb], preferred_element_type=F32)
        ui_ref[...] = jnp.dot(hb[:, ch], bdi_ref[kb], preferred_element_type=F32)
        a_re = jnp.broadcast_to(lbr_ref[:, ln], (nb, SSM_LANE_BLOCK))
        a_im = jnp.broadcast_to(lbi_ref[:, ln], (nb, SSM_LANE_BLOCK))
        s_re, s_im = s0r_ref[:, ln], s0i_ref[:, ln]
        for t in range(tt):
            rows = slice(t * nb, (t + 1) * nb)
            s_re, s_im = ((a_re * s_re - a_im * s_im) + ur_ref[rows, :],
                          (a_re * s_im + a_im * s_re) + ui_ref[rows, :])
            ur_ref[rows, :] = s_re
            ui_ref[rows, :] = s_im
        sr_ref[:, ln] = s_re
        si_ref[:, ln] = s_im
        y = (jnp.dot(ur_ref[...].astype(BF16), cdr_ref[kb], preferred_element_type=F32)
             - jnp.dot(ui_ref[...].astype(BF16), cdi_ref[kb], preferred_element_type=F32))
        y = y + d_ref[:, ch] * h[:, ch]
        z_ref[:, ch] = jax.nn.gelu(y).astype(z_ref.dtype)


def s5_dense(x, g_pre, d_skip, lbr, lbi, bdr, bdi, cdr, cdi, s0r, s0i, nb, tt):
    rows = nb * tt
    full = lambda shp: pl.BlockSpec(shp, lambda: tuple(0 for _ in shp))
    bd = full((SSM_BLOCKS, SSM_CH_BLOCK, SSM_LANE_BLOCK))
    cd = full((SSM_BLOCKS, SSM_LANE_BLOCK, SSM_CH_BLOCK))
    return pl.pallas_call(
        functools.partial(_s5_dense_kernel, nb=nb, tt=tt),
        in_specs=[full((rows, D_MODEL)), full((1, D_MODEL)), full((1, D_MODEL)), full((1, SSM_STATE)),
                  full((1, SSM_STATE)), bd, bd, cd, cd, full((nb, SSM_STATE)), full((nb, SSM_STATE))],
        out_specs=[full((rows, D_MODEL)), full((nb, SSM_STATE)), full((nb, SSM_STATE))],
        out_shape=[jax.ShapeDtypeStruct((rows, D_MODEL), BF16),
                   jax.ShapeDtypeStruct((nb, SSM_STATE), F32), jax.ShapeDtypeStruct((nb, SSM_STATE), F32)],
        scratch_shapes=[pltpu.VMEM((rows, SSM_LANE_BLOCK), F32), pltpu.VMEM((rows, SSM_LANE_BLOCK), F32)],
        compiler_params=pltpu.CompilerParams(vmem_limit_bytes=VMEM_LIMIT),
        name="s5_dense",
    )(x, g_pre.reshape(1, D_MODEL), d_skip.reshape(1, D_MODEL), lbr.reshape(1, SSM_STATE), lbi.reshape(1, SSM_STATE),
      bdr, bdi, cdr, cdi, s0r, s0i)


def _s5_seq_kernel(x_ref, gpre_ref, d_ref, lam_r_ref, lam_i_ref, bdr_ref, bdi_ref, cdr_ref, cdi_ref,
                   z_ref, sr_ref, si_ref, ur_ref, ui_ref, hs_ref, yo_ref):
    tt, pitch = S5_TT, S5_PITCH
    ti = pl.program_id(1)

    @pl.when(ti == 0)
    def _():
        sr_ref[...] = jnp.zeros(sr_ref.shape, F32)
        si_ref[...] = jnp.zeros(si_ref.shape, F32)

    h = _rms(x_ref[...], gpre_ref[...])
    zeros8 = jnp.zeros((SUBLANES, D_MODEL), F32)
    hs_ref[0:SUBLANES, :] = zeros8
    hs_ref[SUBLANES + tt:2 * SUBLANES + tt, :] = zeros8
    hs_ref[SUBLANES:SUBLANES + tt, :] = h
    hb = h.astype(BF16)
    hb_shift = hs_ref[4:tt + 12, :].astype(BF16)

    def even_rows(kb, lt):
        return pl.ds((kb * SUBLANES + 2 * lt) * pitch, tt)

    def odd_rows(kb, lt):
        return pl.ds((kb * SUBLANES + 2 * lt + 1) * pitch - 4, tt + SUBLANES)

    for kb in range(SSM_BLOCKS):
        ch = slice(kb * SSM_CH_BLOCK, (kb + 1) * SSM_CH_BLOCK)
        for u_ref, bd_ref in ((ur_ref, bdr_ref), (ui_ref, bdi_ref)):
            ue = jnp.dot(hb[:, ch], bd_ref[kb, :, 0:SSM_HALF], preferred_element_type=F32)
            uo = jnp.dot(hb_shift[:, ch], bd_ref[kb, :, SSM_HALF:SSM_LANE_BLOCK], preferred_element_type=F32)
            for lt in range(4):
                u_ref[even_rows(kb, lt), :] = ue[:, lt * LANES:(lt + 1) * LANES]
                u_ref[odd_rows(kb, lt), :] = uo[:, lt * LANES:(lt + 1) * LANES]

    blk = lambda q: slice(q * SUBLANES, (q + 1) * SUBLANES)
    a_re = [lam_r_ref[blk(q), :] for q in range(SSM_BLOCKS)]
    a_im = [lam_i_ref[blk(q), :] for q in range(SSM_BLOCKS)]

    def step(t, carry):
        new = []
        for q in range(SSM_BLOCKS):
            s_re, s_im = carry[2 * q], carry[2 * q + 1]
            rows = pl.ds(q * SUBLANES * pitch + t, SUBLANES, stride=pitch)
            n_re = (a_re[q] * s_re - a_im[q] * s_im) + ur_ref[rows, :]
            n_im = (a_re[q] * s_im + a_im[q] * s_re) + ui_ref[rows, :]
            ur_ref[rows, :] = n_re
            ui_ref[rows, :] = n_im
            new += [n_re, n_im]
        return tuple(new)

    init = []
    for q in range(SSM_BLOCKS):
        init += [sr_ref[0, blk(q), :], si_ref[0, blk(q), :]]
    final = lax.fori_loop(0, tt, step, tuple(init), unroll=2)
    for q in range(SSM_BLOCKS):
        sr_ref[0, blk(q), :] = final[2 * q]
        si_ref[0, blk(q), :] = final[2 * q + 1]

    for kb in range(SSM_BLOCKS):
        ch = slice(kb * SSM_CH_BLOCK, (kb + 1) * SSM_CH_BLOCK)

        def gather(u_ref, rows_of):
            return jnp.concatenate([u_ref[rows_of(kb, lt), :] for lt in range(4)], axis=1).astype(BF16)

        y_even = (jnp.dot(gather(ur_ref, even_rows), cdr_ref[kb, 0:SSM_HALF, :], preferred_element_type=F32)
                  - jnp.dot(gather(ui_ref, even_rows), cdi_ref[kb, 0:SSM_HALF, :], preferred_element_type=F32))
        yo_ref[...] = (jnp.dot(gather(ur_ref, odd_rows), cdr_ref[kb, SSM_HALF:SSM_LANE_BLOCK, :],
                               preferred_element_type=F32)
                       - jnp.dot(gather(ui_ref, odd_rows), cdi_ref[kb, SSM_HALF:SSM_LANE_BLOCK, :],
                                 preferred_element_type=F32))
        y = y_even + yo_ref[4:tt + 4, :]
        y = y + d_ref[:, ch] * h[:, ch]
        z_ref[:, ch] = jax.nn.gelu(y).astype(z_ref.dtype)


def s5_seq(x, g_pre, d_skip, lam_r, lam_i, bdr, bdi, cdr, cdi, nseq):
    m = x.shape[0]
    ntt = m // (nseq * S5_TT)
    fixed2 = lambda shp: pl.BlockSpec(shp, lambda si, ti: (0, 0))
    state = pl.BlockSpec((1, SSM_SLABS, LANES), lambda si, ti: (si, 0, 0))
    bd = _resident((SSM_BLOCKS, SSM_CH_BLOCK, SSM_LANE_BLOCK), lambda si, ti: (0, 0, 0))
    cd = _resident((SSM_BLOCKS, SSM_LANE_BLOCK, SSM_CH_BLOCK), lambda si, ti: (0, 0, 0))
    return pl.pallas_call(
        _s5_seq_kernel,
        grid=(nseq, ntt),
        in_specs=[pl.BlockSpec((S5_TT, D_MODEL), lambda si, ti: (si * ntt + ti, 0)),
                  fixed2((1, D_MODEL)), fixed2((1, D_MODEL)), fixed2((SSM_SLABS, LANES)), fixed2((SSM_SLABS, LANES)),
                  bd, bd, cd, cd],
        out_specs=[pl.BlockSpec((S5_TT, D_MODEL), lambda si, ti: (si * ntt + ti, 0)), state, state],
        out_shape=[jax.ShapeDtypeStruct((m, D_MODEL), BF16),
                   jax.ShapeDtypeStruct((nseq, SSM_SLABS, LANES), F32),
                   jax.ShapeDtypeStruct((nseq, SSM_SLABS, LANES), F32)],
        scratch_shapes=[pltpu.VMEM((SSM_SLABS * S5_PITCH, LANES), F32), pltpu.VMEM((SSM_SLABS * S5_PITCH, LANES), F32),
                        pltpu.VMEM((S5_TT + 2 * SUBLANES, D_MODEL), F32),
                        pltpu.VMEM((S5_TT + SUBLANES, SSM_CH_BLOCK), F32)],
        compiler_params=_params(("parallel", "arbitrary")),
        name="s5_seq",
    )(x, g_pre.reshape(1, D_MODEL), d_skip.reshape(1, D_MODEL), lam_r, lam_i, bdr, bdi, cdr, cdi)


def _to_slabs(v):
    lead = v.shape[:-1]
    v = v.reshape(lead + (SSM_BLOCKS, 2, SUBLANES // 2, LANES))
    return v.swapaxes(-3, -2).reshape(lead + (SSM_SLABS, LANES))


def _from_slabs(s):
    lead = s.shape[:-2]
    s = s.reshape(lead + (SSM_BLOCKS, SUBLANES // 2, 2, LANES))
    return s.swapaxes(-3, -2).reshape(lead + (SSM_STATE,))


def _slopes():
    return 2.0 ** (-8.0 * jnp.arange(1, SWA_Q_HEADS + 1, dtype=F32) / SWA_Q_HEADS)


def kernel(x_prompt, x_sample, state_ret, cache_swa_k, cache_swa_v, state_ssm_re, state_ssm_im, state_ffn_conv, norm_mix_pre, norm_mix_post, norm_ffn_pre, norm_ffn_post, w_in_even, w_out_even, swa_sinks, ssm_lam_re, ssm_lam_im, ssm_log_step, ssm_b_re, ssm_b_im, ssm_c_re, ssm_c_im, ssm_d, w_glu, ffn_w_a, ffn_w_g, ffn_conv_w, ffn_conv_b, ffn_w_down):
    pb, pl_len, _ = x_prompt.shape
    sb, sl_len, _ = x_sample.shape
    mp = pb * pl_len
    ms = sb * sl_len
    s_pad = BF16_ROWS
    carry_rows_s = (CONV_W - 1) * sb
    slopes = _slopes()

    xp = x_prompt.reshape(mp, D_MODEL)
    xs = x_sample.transpose(1, 0, 2).reshape(ms, D_MODEL)

    hp = rmsnorm_bf16(xp, norm_mix_pre[0], NORM_TM)
    hs = rmsnorm_bf16(xs, norm_mix_pre[0], ms)

    assert state_ret.shape[0] >= 2
    zero_ret = jnp.zeros((1, pb, RET_HEADS, RET_DK, RET_DV), F32)
    ret_p, ret_s, wk_p, wk_s, wv_p, wv_s = [], [], [], [], [], []
    sre_p, sre_s, sim_p, sim_s, conv_p, conv_s = [], [], [], [], [], []
    w_glu_b = None

    for layer in range(DEPTH):
        i = layer // 2
        if layer % 2 == 0:
            proj, proj_s, w_out_b = in_proj(hp, hs, w_in_even, w_out_even, i, IN_PROJ_TM, IN_PROJ_TN)
            proj = proj.reshape(pb, pl_len, EVEN_IN)
            last_even = i == state_ret.shape[0] - 1
            o_ret, s_ret = retention(proj, zero_ret, 0, ret_p if last_even else [], RET_CHUNK, RET_CHUNK, 1,
                                     RET_CHUNKS_PER_STEP)
            ret_p = s_ret if last_even else ret_p + [s_ret]
            kcol, vcol = KV_COL // LANES, KV_COL // LANES + 1
            blk = lambda c, prev: pl.BlockSpec(
                (1, SWA_BLOCK, LANES),
                (lambda bi, qi: (bi, jnp.maximum(qi - 1, 0), c)) if prev else (lambda bi, qi: (bi, qi, c)))
            o_swa = swa_attention(
                slopes, swa_sinks[i], proj,
                pl.BlockSpec((1, SWA_BLOCK, SWA_WIDTH), lambda bi, qi: (bi, qi, Q_COL // SWA_WIDTH)),
                [proj, proj, proj, proj], [blk(kcol, True), blk(vcol, True), blk(kcol, False), blk(vcol, False)],
                pb, 1, 1, pl_len // SWA_BLOCK, SWA_BLOCK, 0, SWA_BLOCK)
            kv_shape = (pb, WINDOW, SWA_KV_HEADS, SWA_HD)
            wk_p.append(proj[:, pl_len - WINDOW:, KV_COL:KV_COL + KV_WIDTH].reshape(kv_shape))
            wv_p.append(proj[:, pl_len - WINDOW:, KV_COL + KV_WIDTH:KV_COL + 2 * KV_WIDTH].reshape(kv_shape))

            proj_bt = proj_s.reshape(sl_len, sb, EVEN_IN).transpose(1, 0, 2)
            proj_pad = jnp.pad(proj_bt, ((0, 0), (0, s_pad - sl_len), (0, 0)))
            o_ret_s, s_ret_s = retention(proj_pad, state_ret, i, ret_s if last_even else [], sl_len, s_pad,
                                         SAMPLE_SEQS_PER_STEP, 1)
            ret_s = s_ret_s if last_even else ret_s + [s_ret_s]
            win = cache_swa_k.shape[2]
            n_even = cache_swa_k.shape[0]
            ck = cache_swa_k.reshape(n_even, sb, win, KV_WIDTH)
            cv = cache_swa_v.reshape(n_even, sb, win, KV_WIDTH)
            cache_spec = pl.BlockSpec((None, SAMPLE_SEQS_PER_STEP, win, KV_WIDTH), lambda bi, qi: (i, bi, 0, 0))
            cur = lambda c: pl.BlockSpec((SAMPLE_SEQS_PER_STEP, s_pad, LANES), lambda bi, qi: (bi, 0, c))
            o_swa_s = swa_attention(
                slopes, swa_sinks[i], proj_pad,
                pl.BlockSpec((SAMPLE_SEQS_PER_STEP, s_pad, SWA_WIDTH), lambda bi, qi: (bi, 0, Q_COL // SWA_WIDTH)),
                [ck, cv, proj_pad, proj_pad], [cache_spec, cache_spec, cur(kcol), cur(vcol)],
                sb, SAMPLE_SEQS_PER_STEP, SWA_CHAINS, 1, s_pad, win, 0)
            k_new = proj_bt[:, :, KV_COL:KV_COL + KV_WIDTH]
            v_new = proj_bt[:, :, KV_COL + KV_WIDTH:KV_COL + 2 * KV_WIDTH]
            kv_shape_s = (sb, win, SWA_KV_HEADS, SWA_HD)
            wk_s.append(jnp.concatenate([ck[i, :, sl_len:], k_new], axis=1).reshape(kv_shape_s))
            wv_s.append(jnp.concatenate([cv[i, :, sl_len:], v_new], axis=1).reshape(kv_shape_s))
            tb = lambda o: o[:, :sl_len].transpose(1, 0, 2).reshape(ms, -1)

            xp, xs, hp, hs, _ = proj_residual(
                [o_ret.reshape(mp, RET_WIDTH), o_swa.reshape(mp, SWA_WIDTH)], [tb(o_ret_s), tb(o_swa_s)],
                w_out_b, False, xp, xs, norm_mix_post[layer], norm_ffn_pre[layer], OUT_PROJ_TM,
                OUT_PROJ_ROW_PARTS, "out_proj")
        else:
            lbr, lbi, bdr, bdi, cdr, cdi = s5_prep(ssm_lam_re[i], ssm_lam_im[i], ssm_log_step[i], ssm_b_re[i],
                                                   ssm_b_im[i], ssm_c_re[i], ssm_c_im[i])
            lam_r = _to_slabs(lbr.reshape(SSM_STATE))
            lam_i = _to_slabs(lbi.reshape(SSM_STATE))
            zp, s_re, s_im = s5_seq(xp, norm_mix_pre[layer], ssm_d[i], lam_r, lam_i, bdr, bdi, cdr, cdi, pb)
            sre_p.append(_from_slabs(s_re).reshape(pb, SSM_GROUPS, SSM_P))
            sim_p.append(_from_slabs(s_im).reshape(pb, SSM_GROUPS, SSM_P))
            zs, s_re_s, s_im_s = s5_dense(xs, norm_mix_pre[layer], ssm_d[i], lbr, lbi, bdr, bdi, cdr, cdi,
                                          state_ssm_re[i].reshape(sb, SSM_STATE),
                                          state_ssm_im[i].reshape(sb, SSM_STATE), sb, sl_len)
            sre_s.append(s_re_s.reshape(sb, SSM_GROUPS, SSM_P))
            sim_s.append(s_im_s.reshape(sb, SSM_GROUPS, SSM_P))
            xp, xs, hp, hs, _ = proj_residual([zp], [zs], w_glu_b, True, xp, xs, norm_mix_post[layer],
                                              norm_ffn_pre[layer], GLU_TM, GLU_ROW_PARTS, "glu_proj")

        init_s = state_ffn_conv[layer].transpose(1, 0, 2).reshape(carry_rows_s, D_FF)
        act, act_s, tail, tail_s, w_down_b = ffn_up(hp, hs, ffn_w_a, ffn_w_g, ffn_w_down, layer, ffn_conv_w,
                                                    ffn_conv_b, init_s, sb, FFN_UP_TM, FFN_TF, pl_len)
        conv_p.append(tail[:, SUBLANES - (CONV_W - 1):])
        conv_s.append(tail_s.reshape(CONV_W - 1, sb, D_FF).transpose(1, 0, 2))
        next_even = layer + 1 < DEPTH and (layer + 1) % 2 == 0
        next_odd = layer + 1 < DEPTH and (layer + 1) % 2 == 1
        xp, xs, hp, hs, w_cast = proj_residual(
            [act], [act_s], w_down_b, False, xp, xs, norm_ffn_post[layer],
            norm_mix_pre[layer + 1] if next_even else None, FFN_DOWN_TM, FFN_DOWN_ROW_PARTS, "ffn_down",
            cast=(w_glu, (layer + 1) // 2) if next_odd else None)
        if next_odd:
            w_glu_b = w_cast

    y_prompt = xp.reshape(pb, pl_len, D_MODEL)
    y_sample = xs.reshape(sl_len, sb, D_MODEL).transpose(1, 0, 2)
    return (y_prompt, y_sample, ret_p, ret_s, jnp.stack(wk_p), jnp.stack(wk_s),
            jnp.stack(wv_p), jnp.stack(wv_s), jnp.stack(sre_p), jnp.stack(sre_s), jnp.stack(sim_p),
            jnp.stack(sim_s), jnp.stack(conv_p), jnp.stack(conv_s))
```
